```python
import jax, jax.numpy as jnp
from jax import lax
import numpy as np

D_MODEL = 1024
BATCH = 2
SEQ = 8192
DEPTH = 4
DEC_BATCH = 32
DEC_SEQ = 2048
PAST_LEN = 128

GRID_W = 64
N_HEADS = 8
N_KV_HEADS = 2
HEAD_DIM = 64
KV_GROUP = N_HEADS // N_KV_HEADS
ROPE_F = HEAD_DIM // 4
ROPE_THETA = 10000.0
Q_BLOCK = 128
HG_HEADS = 4
HG_DK = 128
HG_DV = 128
HG_CHUNK = 32
ATT_Q = N_HEADS * HEAD_DIM
ATT_KV = N_KV_HEADS * HEAD_DIM
HG_F = HG_HEADS * HG_DK
HG_V = HG_HEADS * HG_DV
BR_W = ATT_Q
D_FF = 2752
N_EXPERTS = 8
TOP_K = 2
D_FF_EXPERT = 3584
N_DENSE = (DEPTH + 1) // 2
N_MOE = DEPTH // 2
EPS = 1e-6
IN_SPLITS = (ATT_Q, ATT_KV, ATT_KV, HG_F, HG_F, HG_F, HG_V, HG_V, D_MODEL, D_MODEL)
IN_COLS = sum(IN_SPLITS)
IN_OFFSETS = tuple(int(o) for o in np.cumsum(IN_SPLITS)[:-1])

kernel_name = 'hybrid_gqa_hgrn2_bidir_encoder'


def rms_norm(x, g):
    xf = x.astype(jnp.float32)
    y = xf * lax.rsqrt(jnp.mean(xf * xf, axis=-1, keepdims=True) + EPS)
    return (y * g.astype(jnp.float32)).astype(x.dtype)


def axial_rope_tables(T):
    rows = T // GRID_W
    row = jnp.repeat(jnp.arange(rows, dtype=jnp.float32), GRID_W)
    col = jnp.tile(jnp.arange(GRID_W, dtype=jnp.float32), rows)
    inv = ROPE_THETA ** (-jnp.arange(ROPE_F, dtype=jnp.float32) / ROPE_F)
    ang = jnp.stack([row[:, None] * inv, col[:, None] * inv], axis=1)
    return jnp.cos(ang), jnp.sin(ang)


def apply_axial_rope(x, cos, sin):
    B, T, H, _ = x.shape
    xr = x.astype(jnp.float32).reshape(B, T, H, 2, 2, ROPE_F)
    x1, x2 = xr[..., 0, :], xr[..., 1, :]
    c = cos[None, :, None]
    s = sin[None, :, None]
    out = jnp.stack([x1 * c - x2 * s, x1 * s + x2 * c], axis=-2)
    return out.reshape(B, T, H, HEAD_DIM).astype(x.dtype)


def gqa_attention(q, k, v):
    B, T = q.shape[:2]
    nblk = T // Q_BLOCK
    qb = q.reshape(B, nblk, Q_BLOCK, N_KV_HEADS, KV_GROUP, HEAD_DIM).transpose(1, 0, 2, 3, 4, 5)
    scale = HEAD_DIM ** -0.5

    def one_block(qblk):
        s = jnp.einsum('bqkgd,bskd->bkgqs', qblk, k, preferred_element_type=jnp.float32) * scale
        p = jax.nn.softmax(s, axis=-1).astype(v.dtype)
        return jnp.einsum('bkgqs,bskd->bqkgd', p, v)

    o = lax.map(one_block, qb)
    return o.transpose(1, 0, 2, 3, 4, 5).reshape(B, T, ATT_Q)


def hgrn2_chunk_scan(q, k, v, logf):
    Bt, T, H, DK = q.shape
    DV = v.shape[-1]
    n = T // HG_CHUNK

    def to_chunks(a):
        return a.reshape(Bt, n, HG_CHUNK, H, a.shape[-1]).transpose(1, 0, 3, 2, 4)

    mask = jnp.tril(jnp.ones((HG_CHUNK, HG_CHUNK), dtype=bool))

    def step(S, inp):
        qc, kc, vc, gc = inp
        b = jnp.cumsum(gc, axis=2)
        o_inter = jnp.einsum('bhtd,bhde->bhte', qc * jnp.exp(b), S)
        diff = b[:, :, :, None, :] - b[:, :, None, :, :]
        decay = jnp.where(mask[:, :, None], jnp.exp(jnp.minimum(diff, 0.0)), 0.0)
        a = jnp.einsum('bhtd,bhtsd,bhsd->bhts', qc, decay, kc)
        o = o_inter + jnp.einsum('bhts,bhse->bhte', a, vc)
        b_last = b[:, :, -1:, :]
        S_new = jnp.exp(b_last[:, :, 0, :])[..., None] * S + jnp.einsum('bhsd,bhse->bhde', kc * jnp.exp(b_last - b), vc)
        return S_new, o

    S0 = jnp.zeros((Bt, H, DK, DV), jnp.float32)
    _, o = lax.scan(step, S0, (to_chunks(q), to_chunks(k), to_chunks(v), to_chunks(logf)))
    return o.transpose(1, 0, 3, 2, 4).reshape(Bt, T, H, DV)


def hgrn2_bidirectional(hq, hff, hfb, hi, hg, lb_f, lb_b, g_out):
    B, T, _ = hq.shape
    shp = (B, T, HG_HEADS, HG_DK)
    q = (jax.nn.silu(hq.astype(jnp.float32)) * (HG_DK ** -0.5)).reshape(shp)
    v = hi.astype(jnp.float32).reshape(B, T, HG_HEADS, HG_DV)

    def forget(hf, lb):
        f = lb + (1.0 - lb) * jax.nn.sigmoid(hf.astype(jnp.float32))
        return (1.0 - f).reshape(shp), jnp.log(f).reshape(shp)

    k_f, g_f = forget(hff, lb_f)
    k_b, g_b = forget(hfb, lb_b)
    flip = lambda a: jnp.flip(a, axis=1)
    o = hgrn2_chunk_scan(jnp.concatenate([q, flip(q)], 0), jnp.concatenate([k_f, flip(k_b)], 0),
                         jnp.concatenate([v, flip(v)], 0), jnp.concatenate([g_f, flip(g_b)], 0))
    o = o[:B] + flip(o[B:])
    o = rms_norm(o, g_out) * jax.nn.silu(hg.astype(jnp.float32)).reshape(B, T, HG_HEADS, HG_DV)
    return o.reshape(B, T, HG_V).astype(hq.dtype)


def token_mixer(h, w_in, g_qk, lb_f, lb_b, g_hgrn, w_branch, w_o, cos, sin):
    B, T, _ = h.shape
    aq, ak, av, hq, hff, hfb, hi, hg, ga, gb = jnp.split(h @ w_in, IN_OFFSETS, axis=-1)
    q = apply_axial_rope(rms_norm(aq.reshape(B, T, N_HEADS, HEAD_DIM), g_qk[0]), cos, sin)
    k = apply_axial_rope(rms_norm(ak.reshape(B, T, N_KV_HEADS, HEAD_DIM), g_qk[1]), cos, sin)
    v = av.reshape(B, T, N_KV_HEADS, HEAD_DIM)
    att = gqa_attention(q, k, v)
    rec = hgrn2_bidirectional(hq, hff, hfb, hi, hg, lb_f, lb_b, g_hgrn)
    merged = jax.nn.sigmoid(ga) * (att @ w_branch[0]) + jax.nn.sigmoid(gb) * (rec @ w_branch[1])
    return merged @ w_o


def swiglu(x, w_in, w_out):
    a, b = jnp.split(x @ w_in, 2, axis=-1)
    return (jax.nn.silu(a) * b) @ w_out


def moe_swiglu(h, w_router, b_router, w_in, w_out):
    B, T, D = h.shape
    xt = h.reshape(-1, D)
    logits = (xt @ w_router).astype(jnp.float32) + b_router.astype(jnp.float32)
    top_val, top_idx = lax.top_k(logits, TOP_K)
    top_w = jax.nn.softmax(top_val, axis=-1)
    comb = jnp.sum(jax.nn.one_hot(top_idx, N_EXPERTS, dtype=jnp.float32) * top_w[..., None], axis=1)
    out = jnp.zeros_like(xt)
    for e in range(N_EXPERTS):
        out = out + comb[:, e:e + 1].astype(xt.dtype) * swiglu(xt, w_in[e], w_out[e])
    return out.reshape(B, T, D)


def encoder_trunk(x, c, w_ada, b_ada, g_norm, w_in, g_qk, lb_logits, g_hgrn, w_branch, w_o,
                  w_ffn_in, w_ffn_out, w_router, b_router, w_moe_in, w_moe_out):
    T = x.shape[1]
    cos, sin = axial_rope_tables(T)
    lb_p = jax.nn.softmax(lb_logits.astype(jnp.float32), axis=1)
    lower_bounds = jnp.cumsum(lb_p, axis=1) - lb_p[:, :1]
    c_act = jax.nn.silu(c)
    for l in range(DEPTH):
        mod = (c_act @ w_ada[l] + b_ada[l])[:, None, :]
        sh1, sc1, gt1, sh2, sc2, gt2 = jnp.split(mod, 6, axis=-1)
        h = rms_norm(x, g_norm[l, 0]) * (1.0 + sc1) + sh1
        y = token_mixer(h, w_in[l], g_qk[l], lower_bounds[0, l], lower_bounds[1, l], g_hgrn[l],
                        w_branch[l], w_o[l], cos, sin)
        x = x + gt1 * rms_norm(y, g_norm[l, 1])
        h = rms_norm(x, g_norm[l, 2]) * (1.0 + sc2) + sh2
        if l % 2 == 0:
            y = swiglu(h, w_ffn_in[l // 2], w_ffn_out[l // 2])
        else:
            y = moe_swiglu(h, w_router[l // 2], b_router[l // 2], w_moe_in[l // 2], w_moe_out[l // 2])
        x = x + gt2 * rms_norm(y, g_norm[l, 3])
    return x


def setup_inputs(seed: int = 0) -> dict:
    key = jax.random.key(seed)
    ks = jax.random.split(key, 24)
    nrm = lambda k, shape, scale: jax.random.normal(k, shape, jnp.float32) * scale
    return {
        'x_prompt': nrm(ks[0], (BATCH, SEQ, D_MODEL), 1.0),
        'x_sample': nrm(ks[1], (DEC_BATCH, DEC_SEQ, D_MODEL), 1.0),
        'c_prompt': nrm(ks[2], (BATCH, D_MODEL), 1.0),
        'c_sample': nrm(ks[3], (DEC_BATCH, D_MODEL), 1.0),
        'w_ada': nrm(ks[4], (DEPTH, D_MODEL, 6 * D_MODEL), 0.5 * D_MODEL ** -0.5),
        'b_ada': nrm(ks[5], (DEPTH, 6 * D_MODEL), 0.01),
        'g_norm': 1.0 + nrm(ks[6], (DEPTH, 4, D_MODEL), 0.01),
        'w_in': nrm(ks[7], (DEPTH, D_MODEL, IN_COLS), D_MODEL ** -0.5),
        'g_qk': 1.0 + nrm(ks[8], (DEPTH, 2, HEAD_DIM), 0.01),
        'lb_logits': nrm(ks[9], (2, DEPTH, HG_F), 1.0),
        'g_hgrn': 1.0 + nrm(ks[10], (DEPTH, HG_DV), 0.01),
        'w_branch': nrm(ks[11], (DEPTH, 2, BR_W, D_MODEL), BR_W ** -0.5),
        'w_o': nrm(ks[12], (DEPTH, D_MODEL, D_MODEL), D_MODEL ** -0.5),
        'w_ffn_in': nrm(ks[13], (N_DENSE, D_MODEL, 2 * D_FF), D_MODEL ** -0.5),
        'w_ffn_out': nrm(ks[14], (N_DENSE, D_FF, D_MODEL), D_FF ** -0.5),
        'w_router': nrm(ks[15], (N_MOE, D_MODEL, N_EXPERTS), D_MODEL ** -0.5),
        'b_router': nrm(ks[16], (N_MOE, N_EXPERTS), 0.01),
        'w_moe_in': nrm(ks[17], (N_MOE, N_EXPERTS, D_MODEL, 2 * D_FF_EXPERT), D_MODEL ** -0.5),
        'w_moe_out': nrm(ks[18], (N_MOE, N_EXPERTS, D_FF_EXPERT, D_MODEL), D_FF_EXPERT ** -0.5),
    }


def reference(x_prompt, x_sample, c_prompt, c_sample, w_ada, b_ada, g_norm, w_in, g_qk, lb_logits,
              g_hgrn, w_branch, w_o, w_ffn_in, w_ffn_out, w_router, b_router, w_moe_in, w_moe_out):
    y_prompt = encoder_trunk(x_prompt, c_prompt, w_ada, b_ada, g_norm, w_in, g_qk, lb_logits, g_hgrn,
                             w_branch, w_o, w_ffn_in, w_ffn_out, w_router, b_router, w_moe_in, w_moe_out)
    y_sample = encoder_trunk(x_sample, c_sample, w_ada, b_ada, g_norm, w_in, g_qk, lb_logits, g_hgrn,
                             w_branch, w_o, w_ffn_in, w_ffn_out, w_router, b_router, w_moe_in, w_moe_out)
    return (y_prompt, y_sample)
```

```python
import functools

import jax
import jax.numpy as jnp
import numpy as np
from jax import lax
from jax.experimental import pallas as pl
from jax.experimental.pallas import tpu as pltpu

F32 = jnp.float32
BF16 = jnp.bfloat16

D_MODEL = 1024
DEPTH = 4
GRID_W = 64
N_HEADS = 8
N_KV_HEADS = 2
HEAD_DIM = 64
KV_GROUP = N_HEADS // N_KV_HEADS
ROPE_F = HEAD_DIM // 4
ROPE_THETA = 10000.0
HG_HEADS = 4
HG_DK = 128
HG_DV = 128
ATT_Q = N_HEADS * HEAD_DIM
ATT_KV = N_KV_HEADS * HEAD_DIM
HG_F = HG_HEADS * HG_DK
D_FF = 2752
N_EXPERTS = 8
D_FF_EXPERT = 3584
EPS = 1e-6

LANES = 128
SUBLANES = 8
VMEM_LIMIT_BYTES = 48 * 1024 * 1024

COL_GA, COL_GB = 0, 1024
COL_HQ, COL_HFF, COL_HFB, COL_HI, COL_HG = 2048, 2560, 3072, 3584, 4096
COL_AQ, COL_AK, COL_AV = 4608, 5120, 5248
IN_COLS = 5376
IN_TILE_N = 768

D_FF_PAD = 2816
FFN_CHUNK_DENSE = 1408
FFN_CHUNK_EXPERT = 896
HG_CHUNK = 128


def _params(sem):
    return pltpu.CompilerParams(dimension_semantics=sem, vmem_limit_bytes=VMEM_LIMIT_BYTES)


def _sigmoid(x):
    return 1.0 / (1.0 + jnp.exp(-x))


def _silu(x):
    return x * _sigmoid(x)


def _rms(x, g):
    ms = jnp.mean(x * x, axis=-1, keepdims=True)
    return x * lax.rsqrt(ms + EPS) * g


def _ada_kernel(c_ref, w_ref, b_ref, o_ref):
    c = c_ref[...]
    o_ref[0] = jnp.dot(_silu(c), w_ref[0], preferred_element_type=F32,
                       precision=lax.Precision.HIGHEST) + b_ref[0]


def _ada(c, w_ada, b_ada):
    bp = c.shape[0]
    n_col = 6 * D_MODEL // D_MODEL
    return pl.pallas_call(
        _ada_kernel,
        grid=(DEPTH, n_col),
        in_specs=[
            pl.BlockSpec((bp, D_MODEL), lambda l, j: (0, 0)),
            pl.BlockSpec((1, D_MODEL, D_MODEL), lambda l, j: (l, 0, j)),
            pl.BlockSpec((1, 1, D_MODEL), lambda l, j: (l, 0, j)),
        ],
        out_specs=pl.BlockSpec((1, bp, D_MODEL), lambda l, j: (l, 0, j)),
        out_shape=jax.ShapeDtypeStruct((DEPTH, bp, 6 * D_MODEL), F32),
        compiler_params=_params(("arbitrary", "arbitrary")),
        name="ada_mod",
    )(c, w_ada, b_ada.reshape(DEPTH, 1, 6 * D_MODEL))


def _inproj_kernel(x_ref, g_ref, sc_ref, sh_ref, w_ref, o_ref, h_scr):
    @pl.when(pl.program_id(1) == 0)
    def _():
        y = _rms(x_ref[...], g_ref[...])
        h_scr[...] = (y * (1.0 + sc_ref[0]) + sh_ref[0]).astype(BF16)

    o_ref[...] = jnp.dot(h_scr[...], w_ref[...], preferred_element_type=F32).astype(BF16)


def _inproj(x, g, sc, sh, w, seq, tm):
    n = x.shape[0]
    per_b = seq // tm
    return pl.pallas_call(
        _inproj_kernel,
        grid=(n // tm, IN_COLS // IN_TILE_N),
        in_specs=[
            pl.BlockSpec((tm, D_MODEL), lambda i, j: (i, 0)),
            pl.BlockSpec((1, D_MODEL), lambda i, j: (0, 0)),
            pl.BlockSpec((1, 1, D_MODEL), lambda i, j: (i // per_b, 0, 0)),
            pl.BlockSpec((1, 1, D_MODEL), lambda i, j: (i // per_b, 0, 0)),
            pl.BlockSpec((D_MODEL, IN_TILE_N), lambda i, j: (0, j)),
        ],
        out_specs=pl.BlockSpec((tm, IN_TILE_N), lambda i, j: (i, j)),
        out_shape=jax.ShapeDtypeStruct((n, IN_COLS), BF16),
        scratch_shapes=[pltpu.VMEM((tm, D_MODEL), BF16)],
        compiler_params=_params(("parallel", "arbitrary")),
        name="in_proj",
    )(x, g, sc, sh, w)


def _rope_kernel(aq_ref, ak_ref, av_ref, cos_ref, sin_ref, gq_ref, gk_ref, bd_ref,
                 qh_ref, ka_ref, kb_ref, va_ref, vb_ref):
    tm = aq_ref.shape[0]
    lane = lax.broadcasted_iota(jnp.int32, (tm, LANES), 1)
    first_half = (lane % (2 * ROPE_F)) < ROPE_F
    low = lane < HEAD_DIM
    cos = cos_ref[...]
    sin = sin_ref[...]
    bd = bd_ref[...]

    def norm_rope(x, g):
        xx = x * x
        hi = xx.astype(BF16)
        lo = (xx - hi.astype(F32)).astype(BF16)
        ms = (jnp.dot(hi, bd, preferred_element_type=F32)
              + jnp.dot(lo, bd, preferred_element_type=F32))
        y = x * lax.rsqrt(ms + EPS) * g
        nxt = pltpu.roll(y, LANES - ROPE_F, 1)
        prv = pltpu.roll(y, ROPE_F, 1)
        return y * cos + jnp.where(first_half, nxt, prv) * sin

    gq = gq_ref[...]
    scale = HEAD_DIM ** -0.5
    for j in range(ATT_Q // LANES):
        r = norm_rope(aq_ref[:, j * LANES:(j + 1) * LANES].astype(F32), gq) * scale
        qh_ref[:, (2 * j) * LANES:(2 * j + 1) * LANES] = jnp.where(low, r, 0.0).astype(BF16)
        qh_ref[:, (2 * j + 1) * LANES:(2 * j + 2) * LANES] = jnp.where(low, 0.0, r).astype(BF16)

    k = norm_rope(ak_ref[...].astype(F32), gk_ref[...])
    ka_ref[...] = k.astype(BF16)
    kb_ref[...] = pltpu.roll(k, HEAD_DIM, 1).astype(BF16)
    v = av_ref[...]
    va_ref[...] = v
    vb_ref[...] = pltpu.roll(v.astype(F32), HEAD_DIM, 1).astype(BF16)


def _rope(proj, cos_t, sin_t, gq, gk, bd, seq, tm):
    n = proj.shape[0]
    per_b = seq // tm
    kv_spec = pl.BlockSpec((tm, LANES), lambda i: (i, 0))
    return pl.pallas_call(
        _rope_kernel,
        grid=(n // tm,),
        in_specs=[
            pl.BlockSpec((tm, ATT_Q), lambda i: (i, COL_AQ // ATT_Q)),
            pl.BlockSpec((tm, LANES), lambda i: (i, COL_AK // LANES)),
            pl.BlockSpec((tm, LANES), lambda i: (i, COL_AV // LANES)),
            pl.BlockSpec((tm, LANES), lambda i: (i % per_b, 0)),
            pl.BlockSpec((tm, LANES), lambda i: (i % per_b, 0)),
            pl.BlockSpec((1, LANES), lambda i: (0, 0)),
            pl.BlockSpec((1, LANES), lambda i: (0, 0)),
            pl.BlockSpec((LANES, LANES), lambda i: (0, 0)),
        ],
        out_specs=[pl.BlockSpec((tm, N_HEADS * LANES), lambda i: (i, 0)),
                   kv_spec, kv_spec, kv_spec, kv_spec],
        out_shape=[jax.ShapeDtypeStruct((n, N_HEADS * LANES), BF16)]
        + [jax.ShapeDtypeStruct((n, LANES), BF16)] * 4,
        compiler_params=_params(("parallel",)),
        name="qk_norm_rope",
    )(proj, proj, proj, cos_t, sin_t, gq, gk, bd)


def _attn_kernel(q_ref, ka_ref, kb_ref, va_ref, vb_ref, o_ref, m_scr, l_scr, acc_scr):
    j = pl.program_id(2)

    @pl.when(j == 0)
    def _():
        m_scr[...] = jnp.full(m_scr.shape, -jnp.inf, F32)
        l_scr[...] = jnp.zeros(l_scr.shape, F32)
        acc_scr[...] = jnp.zeros(acc_scr.shape, F32)

    for h in range(N_HEADS):
        same = (h // KV_GROUP) == (h % 2)
        k = (ka_ref if same else kb_ref)[...]
        v = (va_ref if same else vb_ref)[...]
        q = q_ref[:, h * LANES:(h + 1) * LANES]
        s = lax.dot_general(q, k, (((1,), (1,)), ((), ())), preferred_element_type=F32)
        m_prev = m_scr[h]
        m_new = jnp.maximum(m_prev, jnp.max(s, axis=-1, keepdims=True))
        alpha = jnp.exp(m_prev - m_new)
        p = jnp.exp(s - m_new)
        l_scr[h] = alpha * l_scr[h] + jnp.sum(p, axis=-1, keepdims=True)
        acc_scr[h] = alpha * acc_scr[h] + jnp.dot(p.astype(BF16), v, preferred_element_type=F32)
        m_scr[h] = m_new

    @pl.when(j == pl.num_programs(2) - 1)
    def _():
        tq = q_ref.shape[0]
        low = lax.broadcasted_iota(jnp.int32, (tq, LANES), 1) < HEAD_DIM
        for jj in range(N_HEADS // 2):
            even = acc_scr[2 * jj] / l_scr[2 * jj]
            odd = acc_scr[2 * jj + 1] / l_scr[2 * jj + 1]
            o_ref[:, jj * LANES:(jj + 1) * LANES] = jnp.where(low, even, odd).astype(BF16)


def _attention(qh, ka, kb, va, vb, batch, seq, tq, tk):
    n = qh.shape[0]
    nq, nk = seq // tq, seq // tk
    kv_spec = pl.BlockSpec((tk, LANES), lambda b, i, j: (b * nk + j, 0))
    return pl.pallas_call(
        _attn_kernel,
        grid=(batch, nq, nk),
        in_specs=[pl.BlockSpec((tq, N_HEADS * LANES), lambda b, i, j: (b * nq + i, 0)),
                  kv_spec, kv_spec, kv_spec, kv_spec],
        out_specs=pl.BlockSpec((tq, ATT_Q), lambda b, i, j: (b * nq + i, 0)),
        out_shape=jax.ShapeDtypeStruct((n, ATT_Q), BF16),
        scratch_shapes=[pltpu.VMEM((N_HEADS, tq, 1), F32),
                        pltpu.VMEM((N_HEADS, tq, 1), F32),
                        pltpu.VMEM((N_HEADS, tq, LANES), F32)],
        compiler_params=_params(("parallel", "parallel", "arbitrary")),
        name="gqa_attention",
    )(qh, ka, kb, va, vb)


def _level_ref(b, m, rev):
    c = b.shape[0]
    g = 2 * m
    if g >= SUBLANES:
        b3 = b.reshape(c // g, g, LANES)
        r = m if rev else m - 1
        return jnp.broadcast_to(b3[:, r:r + 1, :], b3.shape).reshape(c, LANES)
    b3 = b.reshape(c // SUBLANES, SUBLANES, LANES)
    sub = lax.broadcasted_iota(jnp.int32, b3.shape, 1)
    out = None
    for p0 in range(0, SUBLANES, g):
        r = p0 + (m if rev else m - 1)
        cand = jnp.broadcast_to(b3[:, r:r + 1, :], b3.shape)
        out = cand if out is None else jnp.where(sub >= p0, cand, out)
    return out.reshape(c, LANES)


def _chunk_cumsum(x, row, rev):
    c = x.shape[0]
    s = 1
    while s < c:
        if rev:
            x = x + jnp.where(row < c - s, pltpu.roll(x, c - s, 0), 0.0)
        else:
            x = x + jnp.where(row >= s, pltpu.roll(x, s, 0), 0.0)
        s *= 2
    return x


def _hgrn_kernel(hq_ref, hf_ref, hi_ref, lbl_ref, o_ref, st_scr, *, rev, layer, chunk):
    c = chunk
    n_chunks = hq_ref.shape[0] // c

    @pl.when(pl.program_id(1) == 0)
    def _():
        st_scr[...] = jnp.zeros(st_scr.shape, F32)

    lbl = lbl_ref[...]
    lbe = jnp.exp(lbl - jnp.max(lbl, axis=0, keepdims=True))
    lbp = lbe / jnp.sum(lbe, axis=0, keepdims=True)
    lb_all = jnp.zeros((1, HG_F), F32)
    for l in range(1, layer + 1):
        lb_all = lb_all + lbp[l:l + 1, :]

    row = lax.broadcasted_iota(jnp.int32, (c, LANES), 0)
    row_a = lax.broadcasted_iota(jnp.int32, (c, c), 0)
    col_a = lax.broadcasted_iota(jnp.int32, (c, c), 1)
    diff_bits = row_a ^ col_a
    q_scale = HG_DK ** -0.5

    def body(ci, carry):
        cc = (n_chunks - 1 - ci) if rev else ci
        r0 = pl.multiple_of(cc * c, c)
        for h in range(HG_HEADS):
            cols = slice(h * LANES, (h + 1) * LANES)
            hq = hq_ref[pl.ds(r0, c), cols].astype(F32)
            hf = hf_ref[pl.ds(r0, c), cols].astype(F32)
            v = hi_ref[pl.ds(r0, c), cols]
            lb = lb_all[:, cols]
            q = _silu(hq) * q_scale
            f = lb + (1.0 - lb) * _sigmoid(hf)
            k = 1.0 - f
            b = _chunk_cumsum(jnp.log(f), row, rev)
            st = st_scr[h]
            o = lax.dot_general((q * jnp.exp(b)).astype(BF16), st.astype(BF16),
                                (((1,), (1,)), ((), ())), preferred_element_type=F32)
            a = jnp.zeros((c, c), F32)
            m = c // 2
            while m >= 1:
                e = jnp.exp(-jnp.abs(b - _level_ref(b, m, rev)))
                upper = (row & m) != 0
                q_rows = (~upper) if rev else upper
                qe = jnp.where(q_rows, q * e, 0.0).astype(BF16)
                ke = jnp.where(q_rows, 0.0, k * e).astype(BF16)
                al = lax.dot_general(qe, ke, (((1,), (1,)), ((), ())), preferred_element_type=F32)
                a = a + jnp.where(diff_bits < 2 * m, al, 0.0)
                m //= 2
            diag = jnp.sum(q * k, axis=-1, keepdims=True)
            o = o + jnp.dot(a.astype(BF16), v, preferred_element_type=F32) + diag * v.astype(F32)
            o_ref[pl.ds(r0, c), cols] = o.astype(o_ref.dtype)
            b_end = b[0:1, :] if rev else b[c - 1:c, :]
            k_end = (k * jnp.exp(b_end - b)).astype(BF16)
            upd = lax.dot_general(v, k_end, (((0,), (0,)), ((), ())), preferred_element_type=F32)
            st_scr[h] = st * jnp.exp(b_end) + upd
        return carry

    lax.fori_loop(0, n_chunks, body, 0)


def _hgrn(proj, lb_logits_dir, batch, seq, tt, rev, layer):
    n = proj.shape[0]
    nt = seq // tt
    col_f = (COL_HFB if rev else COL_HFF) // HG_F

    def rows(b, i):
        return b * nt + ((nt - 1 - i) if rev else i)

    return pl.pallas_call(
        functools.partial(_hgrn_kernel, rev=rev, layer=layer, chunk=min(HG_CHUNK, tt)),
        grid=(batch, nt),
        in_specs=[
            pl.BlockSpec((tt, HG_F), lambda b, i: (rows(b, i), COL_HQ // HG_F)),
            pl.BlockSpec((tt, HG_F), lambda b, i: (rows(b, i), col_f)),
            pl.BlockSpec((tt, HG_F), lambda b, i: (rows(b, i), COL_HI // HG_F)),
            pl.BlockSpec((DEPTH, HG_F), lambda b, i: (0, 0)),
        ],
        out_specs=pl.BlockSpec((tt, HG_F), lambda b, i: (rows(b, i), 0)),
        out_shape=jax.ShapeDtypeStruct((n, HG_F), BF16),
        scratch_shapes=[pltpu.VMEM((HG_HEADS, HG_DV, HG_DK), F32)],
        compiler_params=_params(("parallel", "arbitrary")),
        name="hgrn2_bwd" if rev else "hgrn2_fwd",
    )(proj, proj, proj, lb_logits_dir)


def _mixout_kernel(att_ref, of_ref, ob_ref, hg_ref, ga_ref, gb_ref, x_ref, gt_ref, ghg_ref,
                   gn_ref, wa_ref, wb_ref, wo_ref, o_ref):
    ghg = ghg_ref[...]
    recs = []
    for h in range(HG_HEADS):
        cols = slice(h * LANES, (h + 1) * LANES)
        o = of_ref[:, cols].astype(F32) + ob_ref[:, cols].astype(F32)
        recs.append((_rms(o, ghg) * _silu(hg_ref[:, cols].astype(F32))).astype(BF16))
    rec = jnp.concatenate(recs, axis=1)
    pa = jnp.dot(att_ref[...], wa_ref[...], preferred_element_type=F32)
    pb = jnp.dot(rec, wb_ref[...], preferred_element_type=F32)
    merged = (_sigmoid(ga_ref[...].astype(F32)) * pa + _sigmoid(gb_ref[...].astype(F32)) * pb)
    y = jnp.dot(merged.astype(BF16), wo_ref[...], preferred_element_type=F32)
    o_ref[...] = x_ref[...] + gt_ref[0] * _rms(y, gn_ref[...])


def _mixout(att, o_f, o_b, proj, x, gt, g_hgrn, g_norm, w_a, w_b, w_o, seq, tm):
    n = x.shape[0]
    per_b = seq // tm
    half = pl.BlockSpec((tm, ATT_Q), lambda i: (i, 0))
    const = lambda shape: pl.BlockSpec(shape, lambda i: (0,) * len(shape))
    return pl.pallas_call(
        _mixout_kernel,
        grid=(n // tm,),
        in_specs=[
            half, half, half,
            pl.BlockSpec((tm, HG_F), lambda i: (i, COL_HG // HG_F)),
            pl.BlockSpec((tm, D_MODEL), lambda i: (i, COL_GA // D_MODEL)),
            pl.BlockSpec((tm, D_MODEL), lambda i: (i, COL_GB // D_MODEL)),
            pl.BlockSpec((tm, D_MODEL), lambda i: (i, 0)),
            pl.BlockSpec((1, 1, D_MODEL), lambda i: (i // per_b, 0, 0)),
            const((1, LANES)), const((1, D_MODEL)),
            const((ATT_Q, D_MODEL)), const((HG_F, D_MODEL)), const((D_MODEL, D_MODEL)),
        ],
        out_specs=pl.BlockSpec((tm, D_MODEL), lambda i: (i, 0)),
        out_shape=jax.ShapeDtypeStruct((n, D_MODEL), F32),
        compiler_params=_params(("parallel",)),
        name="mixer_out",
    )(att, o_f, o_b, proj, proj, proj, x, gt, g_hgrn, g_norm, w_a, w_b, w_o)


def _router_kernel(x_ref, g_ref, sc_ref, sh_ref, wr_ref, br_ref, comb_ref):
    h = _rms(x_ref[...], g_ref[...]) * (1.0 + sc_ref[0]) + sh_ref[0]
    logits = jnp.dot(h, wr_ref[...], preferred_element_type=F32,
                     precision=lax.Precision.HIGHEST) + br_ref[...]
    lane = lax.broadcasted_iota(jnp.int32, logits.shape, 1).astype(F32)
    logits = jnp.where(lane < N_EXPERTS, logits, -jnp.inf)
    m1 = jnp.max(logits, axis=-1, keepdims=True)
    i1 = jnp.min(jnp.where(logits == m1, lane, float(LANES)), axis=-1, keepdims=True)
    rest = jnp.where(lane == i1, -jnp.inf, logits)
    m2 = jnp.max(rest, axis=-1, keepdims=True)
    i2 = jnp.min(jnp.where(rest == m2, lane, float(LANES)), axis=-1, keepdims=True)
    e2 = jnp.exp(m2 - m1)
    w1 = 1.0 / (1.0 + e2)
    w2 = e2 / (1.0 + e2)
    comb_ref[...] = jnp.where(lane == i1, w1, 0.0) + jnp.where(lane == i2, w2, 0.0)


def _router(x, g, sc, sh, w_r, b_r, seq, tm):
    n = x.shape[0]
    per_b = seq // tm
    return pl.pallas_call(
        _router_kernel,
        grid=(n // tm,),
        in_specs=[
            pl.BlockSpec((tm, D_MODEL), lambda i: (i, 0)),
            pl.BlockSpec((1, D_MODEL), lambda i: (0, 0)),
            pl.BlockSpec((1, 1, D_MODEL), lambda i: (i // per_b, 0, 0)),
            pl.BlockSpec((1, 1, D_MODEL), lambda i: (i // per_b, 0, 0)),
            pl.BlockSpec((D_MODEL, LANES), lambda i: (0, 0)),
            pl.BlockSpec((1, LANES), lambda i: (0, 0)),
        ],
        out_specs=pl.BlockSpec((tm, LANES), lambda i: (i, 0)),
        out_shape=jax.ShapeDtypeStruct((n, LANES), F32),
        compiler_params=_params(("parallel",)),
        name="router_top2",
    )(x, g, sc, sh, w_r, b_r)


def _ffn_kernel(*refs, use_comb):
    if use_comb:
        (x_ref, g_ref, sc_ref, sh_ref, gt_ref, gn_ref, comb_ref, wa_ref, wb_ref, wo_ref,
         o_ref, h_scr, acc_scr) = refs
    else:
        (x_ref, g_ref, sc_ref, sh_ref, gt_ref, gn_ref, wa_ref, wb_ref, wo_ref,
         o_ref, h_scr, acc_scr) = refs
    e = pl.program_id(1)
    f = pl.program_id(2)

    @pl.when((e == 0) & (f == 0))
    def _():
        y = _rms(x_ref[...], g_ref[...])
        h_scr[...] = (y * (1.0 + sc_ref[0]) + sh_ref[0]).astype(BF16)
        acc_scr[...] = jnp.zeros(acc_scr.shape, F32)

    h = h_scr[...]
    a = jnp.dot(h, wa_ref[0], preferred_element_type=F32)
    b = jnp.dot(h, wb_ref[0], preferred_element_type=F32)
    y = jnp.dot((_silu(a) * b).astype(BF16), wo_ref[0], preferred_element_type=F32)
    if use_comb:
        comb = comb_ref[...]
        lane = lax.broadcasted_iota(jnp.int32, comb.shape, 1)
        y = jnp.sum(jnp.where(lane == e, comb, 0.0), axis=-1, keepdims=True) * y
    acc_scr[...] += y

    @pl.when((e == pl.num_programs(1) - 1) & (f == pl.num_programs(2) - 1))
    def _():
        o_ref[...] = x_ref[...] + gt_ref[0] * _rms(acc_scr[...], gn_ref[...])


def _ffn(x, g, sc, sh, gt, gn, comb, w_in, w_out, seq, tm, chunk):
    n = x.shape[0]
    per_b = seq // tm
    n_e, _, two_ff = w_in.shape
    n_fc = two_ff // 2 // chunk
    use_comb = comb is not None
    vec = pl.BlockSpec((1, D_MODEL), lambda i, e, f: (0, 0))
    mod = pl.BlockSpec((1, 1, D_MODEL), lambda i, e, f: (i // per_b, 0, 0))
    in_specs = [pl.BlockSpec((tm, D_MODEL), lambda i, e, f: (i, 0)), vec, mod, mod, mod, vec]
    args = [x, g, sc, sh, gt, gn]
    if use_comb:
        in_specs.append(pl.BlockSpec((tm, LANES), lambda i, e, f: (i, 0)))
        args.append(comb)
    in_specs += [
        pl.BlockSpec((1, D_MODEL, chunk), lambda i, e, f: (e, 0, f)),
        pl.BlockSpec((1, D_MODEL, chunk), lambda i, e, f: (e, 0, n_fc + f)),
        pl.BlockSpec((1, chunk, D_MODEL), lambda i, e, f: (e, f, 0)),
    ]
    args += [w_in, w_in, w_out]
    return pl.pallas_call(
        functools.partial(_ffn_kernel, use_comb=use_comb),
        grid=(n // tm, n_e, n_fc),
        in_specs=in_specs,
        out_specs=pl.BlockSpec((tm, D_MODEL), lambda i, e, f: (i, 0)),
        out_shape=jax.ShapeDtypeStruct((n, D_MODEL), F32),
        scratch_shapes=[pltpu.VMEM((tm, D_MODEL), BF16), pltpu.VMEM((tm, D_MODEL), F32)],
        compiler_params=_params(("parallel", "arbitrary", "arbitrary")),
        name="moe_ffn" if use_comb else "dense_ffn",
    )(*args)


def _rope_tables(seq):
    t = jnp.arange(seq, dtype=jnp.int32)
    row = (t // GRID_W).astype(F32)
    col = (t % GRID_W).astype(F32)
    inv = ROPE_THETA ** (-jnp.arange(ROPE_F, dtype=F32) / ROPE_F)
    ang_r = row[:, None] * inv
    ang_c = col[:, None] * inv
    cos = jnp.concatenate([jnp.cos(ang_r)] * 2 + [jnp.cos(ang_c)] * 2, axis=1)
    sin = jnp.concatenate([-jnp.sin(ang_r), jnp.sin(ang_r), -jnp.sin(ang_c), jnp.sin(ang_c)], axis=1)
    return jnp.tile(cos, (1, 2)), jnp.tile(sin, (1, 2))


def _prep_weights(w_in, g_qk, w_branch, w_o, w_ffn_in, w_ffn_out, w_router, b_router,
                  w_moe_in, w_moe_out):
    o_aq, o_ak, o_av, o_hq = 0, ATT_Q, ATT_Q + ATT_KV, ATT_Q + 2 * ATT_KV
    o_ga = o_hq + 5 * HG_F
    w_in_p = jnp.concatenate(
        [w_in[:, :, o_ga:], w_in[:, :, o_hq:o_ga], w_in[:, :, o_aq:o_hq]], axis=2).astype(BF16)
    pad_c = D_FF_PAD - D_FF
    w_fa = jnp.pad(w_ffn_in[:, :, :D_FF], ((0, 0), (0, 0), (0, pad_c)))
    w_fb = jnp.pad(w_ffn_in[:, :, D_FF:], ((0, 0), (0, 0), (0, pad_c)))
    w_ffn_in_p = jnp.concatenate([w_fa, w_fb], axis=2).astype(BF16)
    w_ffn_out_p = jnp.pad(w_ffn_out, ((0, 0), (0, pad_c), (0, 0))).astype(BF16)
    w_r_p = jnp.pad(w_router, ((0, 0), (0, 0), (0, LANES - N_EXPERTS)))
    b_r_p = jnp.pad(b_router, ((0, 0), (0, LANES - N_EXPERTS)))[:, None, :]
    g_q = jnp.tile(g_qk[:, 0, :], (1, LANES // HEAD_DIM))[:, None, :]
    g_k = jnp.tile(g_qk[:, 1, :], (1, LANES // HEAD_DIM))[:, None, :]
    return dict(w_in=w_in_p, w_branch=w_branch.astype(BF16), w_o=w_o.astype(BF16),
                w_ffn_in=w_ffn_in_p, w_ffn_out=w_ffn_out_p, w_r=w_r_p, b_r=b_r_p,
                w_moe_in=w_moe_in.astype(BF16), w_moe_out=w_moe_out.astype(BF16),
                g_q=g_q, g_k=g_k)


def _tile(seq, want):
    return min(seq, want)


def _trunk(x3, mods, g_norm, lb_logits, g_hgrn, wts):
    batch, seq, _ = x3.shape
    x = x3.reshape(batch * seq, D_MODEL)
    cos_t, sin_t = _rope_tables(seq)
    idx = np.arange(LANES)
    bd = jnp.asarray((idx[:, None] // HEAD_DIM == idx[None, :] // HEAD_DIM) / HEAD_DIM, BF16)
    tm_big = _tile(seq, 1024)
    tm = _tile(seq, 512)
    for l in range(DEPTH):
        mod = mods[l]
        sh1, sc1, gt1, sh2, sc2, gt2 = [m[:, None, :] for m in jnp.split(mod, 6, axis=-1)]
        gn = g_norm[l][:, None, :]
        proj = _inproj(x, gn[0], sc1, sh1, wts["w_in"][l], seq, tm_big)
        qh, ka, kb, va, vb = _rope(proj, cos_t, sin_t, wts["g_q"][l], wts["g_k"][l], bd, seq, tm)
        att = _attention(qh, ka, kb, va, vb, batch, seq, tm, tm)
        o_f = _hgrn(proj, lb_logits[0], batch, seq, tm, False, l)
        o_b = _hgrn(proj, lb_logits[1], batch, seq, tm, True, l)
        x = _mixout(att, o_f, o_b, proj, x, gt1, g_hgrn[l][None, :], gn[1],
                    wts["w_branch"][l, 0], wts["w_branch"][l, 1], wts["w_o"][l], seq, tm)
        if l % 2 == 0:
            x = _ffn(x, gn[2], sc2, sh2, gt2, gn[3], None,
                     wts["w_ffn_in"][l // 2][None], wts["w_ffn_out"][l // 2][None],
                     seq, tm, FFN_CHUNK_DENSE)
        else:
            comb = _router(x, gn[2], sc2, sh2, wts["w_r"][l // 2], wts["b_r"][l // 2], seq, tm)
            x = _ffn(x, gn[2], sc2, sh2, gt2, gn[3], comb,
                     wts["w_moe_in"][l // 2], wts["w_moe_out"][l // 2],
                     seq, tm, FFN_CHUNK_EXPERT)
    return x.reshape(batch, seq, D_MODEL)


def kernel(x_prompt, x_sample, c_prompt, c_sample, w_ada, b_ada, g_norm, w_in, g_qk, lb_logits,
           g_hgrn, w_branch, w_o, w_ffn_in, w_ffn_out, w_router, b_router, w_moe_in, w_moe_out):
    wts = _prep_weights(w_in, g_qk, w_branch, w_o, w_ffn_in, w_ffn_out, w_router, b_router,
                        w_moe_in, w_moe_out)
    n_p, n_s = c_prompt.shape[0], c_sample.shape[0]
    c_all = jnp.concatenate([c_prompt, c_sample], axis=0)
    pad = (-c_all.shape[0]) % SUBLANES
    c_all = jnp.pad(c_all, ((0, pad), (0, 0)))
    mods = _ada(c_all, w_ada, b_ada)
    y_p = _trunk(x_prompt, mods[:, :n_p], g_norm, lb_logits, g_hgrn, wts)
    y_s = _trunk(x_sample, mods[:, n_p:n_p + n_s], g_norm, lb_logits, g_hgrn, wts)
    return (y_p, y_s)
```

```python
import functools

import jax
import jax.numpy as jnp
import numpy as np
from jax import lax
from jax.experimental import pallas as pl
from jax.experimental.pallas import tpu as pltpu

F32 = jnp.float32
BF16 = jnp.bfloat16

D_MODEL = 1024
DEPTH = 4
GRID_W = 64
N_HEADS = 8
N_KV_HEADS = 2
HEAD_DIM = 64
KV_GROUP = N_HEADS // N_KV_HEADS
ROPE_F = HEAD_DIM // 4
ROPE_THETA = 10000.0
HG_HEADS = 4
HG_DK = 128
HG_DV = 128
ATT_Q = N_HEADS * HEAD_DIM
ATT_KV = N_KV_HEADS * HEAD_DIM
HG_F = HG_HEADS * HG_DK
D_FF = 2752
N_EXPERTS = 8
D_FF_EXPERT = 3584
EPS = 1e-6

LANES = 128
SUBLANES = 8
VMEM_LIMIT_BYTES = 48 * 1024 * 1024

COL_GA, COL_GB = 0, 1024
COL_HQ, COL_HFF, COL_HFB, COL_HI, COL_HG = 2048, 2560, 3072, 3584, 4096
COL_AQ, COL_AK, COL_AV = 4608, 5120, 5248
IN_COLS = 5376
IN_TILE_N = 768

D_FF_PAD = 2816
FFN_CHUNK_DENSE = 1408
FFN_CHUNK_EXPERT = 896
HG_CHUNK = 128
MOE_TILE = 512
SEL_E1, SEL_E2, SEL_W1, SEL_W2 = 0, 1, 2, 3


def _params(sem):
    return pltpu.CompilerParams(dimension_semantics=sem, vmem_limit_bytes=VMEM_LIMIT_BYTES)


def _sigmoid(x):
    return 1.0 / (1.0 + jnp.exp(-x))


def _silu(x):
    return x * _sigmoid(x)


def _rms(x, g):
    ms = jnp.mean(x * x, axis=-1, keepdims=True)
    return x * lax.rsqrt(ms + EPS) * g


def _ada_kernel(c_ref, w_ref, b_ref, o_ref):
    c = c_ref[...]
    o_ref[0] = jnp.dot(_silu(c), w_ref[0], preferred_element_type=F32,
                       precision=lax.Precision.HIGHEST) + b_ref[0]


def _ada(c, w_ada, b_ada):
    bp = c.shape[0]
    n_col = 6 * D_MODEL // D_MODEL
    return pl.pallas_call(
        _ada_kernel,
        grid=(DEPTH, n_col),
        in_specs=[
            pl.BlockSpec((bp, D_MODEL), lambda l, j: (0, 0)),
            pl.BlockSpec((1, D_MODEL, D_MODEL), lambda l, j: (l, 0, j)),
            pl.BlockSpec((1, 1, D_MODEL), lambda l, j: (l, 0, j)),
        ],
        out_specs=pl.BlockSpec((1, bp, D_MODEL), lambda l, j: (l, 0, j)),
        out_shape=jax.ShapeDtypeStruct((DEPTH, bp, 6 * D_MODEL), F32),
        compiler_params=_params(("arbitrary", "arbitrary")),
        name="ada_mod",
    )(c, w_ada, b_ada.reshape(DEPTH, 1, 6 * D_MODEL))


def _inproj_kernel(x_ref, g_ref, sc_ref, sh_ref, w_ref, o_ref, h_scr):
    @pl.when(pl.program_id(1) == 0)
    def _():
        y = _rms(x_ref[...], g_ref[...])
        h_scr[...] = (y * (1.0 + sc_ref[0]) + sh_ref[0]).astype(BF16)

    o_ref[...] = jnp.dot(h_scr[...], w_ref[...], preferred_element_type=F32).astype(BF16)


def _inproj(x, g, sc, sh, w, seq, tm):
    n = x.shape[0]
    per_b = seq // tm
    return pl.pallas_call(
        _inproj_kernel,
        grid=(n // tm, IN_COLS // IN_TILE_N),
        in_specs=[
            pl.BlockSpec((tm, D_MODEL), lambda i, j: (i, 0)),
            pl.BlockSpec((1, D_MODEL), lambda i, j: (0, 0)),
            pl.BlockSpec((1, 1, D_MODEL), lambda i, j: (i // per_b, 0, 0)),
            pl.BlockSpec((1, 1, D_MODEL), lambda i, j: (i // per_b, 0, 0)),
            pl.BlockSpec((D_MODEL, IN_TILE_N), lambda i, j: (0, j)),
        ],
        out_specs=pl.BlockSpec((tm, IN_TILE_N), lambda i, j: (i, j)),
        out_shape=jax.ShapeDtypeStruct((n, IN_COLS), BF16),
        scratch_shapes=[pltpu.VMEM((tm, D_MODEL), BF16)],
        compiler_params=_params(("parallel", "arbitrary")),
        name="in_proj",
    )(x, g, sc, sh, w)


def _rope_kernel(aq_ref, ak_ref, av_ref, cos_ref, sin_ref, gq_ref, gk_ref, bd_ref,
                 qh_ref, ka_ref, kb_ref, vta_ref, vtb_ref):
    tm = aq_ref.shape[0]
    lane = lax.broadcasted_iota(jnp.int32, (tm, LANES), 1)
    first_half = (lane % (2 * ROPE_F)) < ROPE_F
    low = lane < HEAD_DIM
    cos = cos_ref[...]
    sin = sin_ref[...]
    bd = bd_ref[...]

    def norm_rope(x, g):
        xx = x * x
        hi = xx.astype(BF16)
        lo = (xx - hi.astype(F32)).astype(BF16)
        ms = (jnp.dot(hi, bd, preferred_element_type=F32)
              + jnp.dot(lo, bd, preferred_element_type=F32))
        y = x * lax.rsqrt(ms + EPS) * g
        nxt = pltpu.roll(y, LANES - ROPE_F, 1)
        prv = pltpu.roll(y, ROPE_F, 1)
        return y * cos + jnp.where(first_half, nxt, prv) * sin

    gq = gq_ref[...]
    scale = HEAD_DIM ** -0.5 * float(np.log2(np.e))
    for j in range(ATT_Q // LANES):
        r = norm_rope(aq_ref[:, j * LANES:(j + 1) * LANES].astype(F32), gq) * scale
        qh_ref[:, (2 * j) * LANES:(2 * j + 1) * LANES] = jnp.where(low, r, 0.0).astype(BF16)
        qh_ref[:, (2 * j + 1) * LANES:(2 * j + 2) * LANES] = jnp.where(low, 0.0, r).astype(BF16)

    k = norm_rope(ak_ref[...].astype(F32), gk_ref[...])
    ka_ref[...] = k.astype(BF16)
    kb_ref[...] = pltpu.roll(k, HEAD_DIM, 1).astype(BF16)
    v = av_ref[...].astype(F32)
    vta_ref[...] = v.T.astype(BF16)
    vtb_ref[...] = pltpu.roll(v, HEAD_DIM, 1).T.astype(BF16)


def _rope(proj, cos_t, sin_t, gq, gk, bd, seq, tm):
    n = proj.shape[0]
    per_b = seq // tm
    kv_spec = pl.BlockSpec((tm, LANES), lambda i: (i, 0))
    vt_spec = pl.BlockSpec((LANES, tm), lambda i: (0, i))
    return pl.pallas_call(
        _rope_kernel,
        grid=(n // tm,),
        in_specs=[
            pl.BlockSpec((tm, ATT_Q), lambda i: (i, COL_AQ // ATT_Q)),
            pl.BlockSpec((tm, LANES), lambda i: (i, COL_AK // LANES)),
            pl.BlockSpec((tm, LANES), lambda i: (i, COL_AV // LANES)),
            pl.BlockSpec((tm, LANES), lambda i: (i % per_b, 0)),
            pl.BlockSpec((tm, LANES), lambda i: (i % per_b, 0)),
            pl.BlockSpec((1, LANES), lambda i: (0, 0)),
            pl.BlockSpec((1, LANES), lambda i: (0, 0)),
            pl.BlockSpec((LANES, LANES), lambda i: (0, 0)),
        ],
        out_specs=[pl.BlockSpec((tm, N_HEADS * LANES), lambda i: (i, 0)),
                   kv_spec, kv_spec, vt_spec, vt_spec],
        out_shape=[jax.ShapeDtypeStruct((n, N_HEADS * LANES), BF16)]
        + [jax.ShapeDtypeStruct((n, LANES), BF16)] * 2
        + [jax.ShapeDtypeStruct((LANES, n), BF16)] * 2,
        compiler_params=_params(("parallel",)),
        name="qk_norm_rope",
    )(proj, proj, proj, cos_t, sin_t, gq, gk, bd)


def _attn_kernel(q_ref, ka_ref, kb_ref, vta_ref, vtb_ref, o_ref, m_scr, l_scr, acc_scr):
    j = pl.program_id(2)

    @pl.when(j == 0)
    def _():
        m_scr[...] = jnp.full(m_scr.shape, -jnp.inf, F32)
        l_scr[...] = jnp.zeros(l_scr.shape, F32)
        acc_scr[...] = jnp.zeros(acc_scr.shape, F32)

    for h in range(N_HEADS):
        same = (h // KV_GROUP) == (h % 2)
        k = (ka_ref if same else kb_ref)[...]
        vt = (vta_ref if same else vtb_ref)[...]
        q = q_ref[:, h * LANES:(h + 1) * LANES]
        st = lax.dot_general(k, q, (((1,), (1,)), ((), ())), preferred_element_type=F32)
        m_prev = m_scr[h]
        m_new = jnp.maximum(m_prev, jnp.max(st, axis=0, keepdims=True))
        alpha = jnp.exp2(m_prev - m_new)
        pt = jnp.exp2(st - m_new)
        l_scr[h] = alpha * l_scr[h] + jnp.sum(pt, axis=0, keepdims=True)
        acc_scr[h] = alpha * acc_scr[h] + jnp.dot(vt, pt.astype(BF16), preferred_element_type=F32)
        m_scr[h] = m_new

    @pl.when(j == pl.num_programs(2) - 1)
    def _():
        tq = q_ref.shape[0]
        low = lax.broadcasted_iota(jnp.int32, (tq, LANES), 1) < HEAD_DIM
        for jj in range(N_HEADS // 2):
            even = (acc_scr[2 * jj] / l_scr[2 * jj]).T
            odd = (acc_scr[2 * jj + 1] / l_scr[2 * jj + 1]).T
            o_ref[:, jj * LANES:(jj + 1) * LANES] = jnp.where(low, even, odd).astype(BF16)


def _attention(qh, ka, kb, vta, vtb, batch, seq, tq, tk):
    n = qh.shape[0]
    nq, nk = seq // tq, seq // tk
    k_spec = pl.BlockSpec((tk, LANES), lambda b, i, j: (b * nk + j, 0))
    vt_spec = pl.BlockSpec((LANES, tk), lambda b, i, j: (0, b * nk + j))
    return pl.pallas_call(
        _attn_kernel,
        grid=(batch, nq, nk),
        in_specs=[pl.BlockSpec((tq, N_HEADS * LANES), lambda b, i, j: (b * nq + i, 0)),
                  k_spec, k_spec, vt_spec, vt_spec],
        out_specs=pl.BlockSpec((tq, ATT_Q), lambda b, i, j: (b * nq + i, 0)),
        out_shape=jax.ShapeDtypeStruct((n, ATT_Q), BF16),
        scratch_shapes=[pltpu.VMEM((N_HEADS, 1, tq), F32),
                        pltpu.VMEM((N_HEADS, 1, tq), F32),
                        pltpu.VMEM((N_HEADS, LANES, tq), F32)],
        compiler_params=_params(("parallel", "parallel", "arbitrary")),
        name="gqa_attention",
    )(qh, ka, kb, vta, vtb)


def _level_ref(b, m, rev):
    c = b.shape[0]
    g = 2 * m
    if g >= SUBLANES:
        b3 = b.reshape(c // g, g, LANES)
        r = m if rev else m - 1
        return jnp.broadcast_to(b3[:, r:r + 1, :], b3.shape).reshape(c, LANES)
    b3 = b.reshape(c // SUBLANES, SUBLANES, LANES)
    sub = lax.broadcasted_iota(jnp.int32, b3.shape, 1)
    out = None
    for p0 in range(0, SUBLANES, g):
        r = p0 + (m if rev else m - 1)
        cand = jnp.broadcast_to(b3[:, r:r + 1, :], b3.shape)
        out = cand if out is None else jnp.where(sub >= p0, cand, out)
    return out.reshape(c, LANES)


def _chunk_cumsum(x, row, rev):
    c = x.shape[0]
    s = 1
    while s < c:
        if rev:
            x = x + jnp.where(row < c - s, pltpu.roll(x, c - s, 0), 0.0)
        else:
            x = x + jnp.where(row >= s, pltpu.roll(x, s, 0), 0.0)
        s *= 2
    return x


def _hgrn_kernel(hq_ref, hf_ref, hi_ref, lbl_ref, o_ref, st_scr, *, rev, layer, chunk):
    c = chunk
    n_chunks = hq_ref.shape[0] // c

    @pl.when(pl.program_id(1) == 0)
    def _():
        st_scr[...] = jnp.zeros(st_scr.shape, F32)

    lbl = lbl_ref[...]
    lbe = jnp.exp(lbl - jnp.max(lbl, axis=0, keepdims=True))
    lbp = lbe / jnp.sum(lbe, axis=0, keepdims=True)
    lb_all = jnp.zeros((1, HG_F), F32)
    for l in range(1, layer + 1):
        lb_all = lb_all + lbp[l:l + 1, :]

    row = lax.broadcasted_iota(jnp.int32, (c, LANES), 0)
    row_a = lax.broadcasted_iota(jnp.int32, (c, c), 0)
    col_a = lax.broadcasted_iota(jnp.int32, (c, c), 1)
    diff_bits = row_a ^ col_a
    q_scale = HG_DK ** -0.5

    def body(ci, carry):
        cc = (n_chunks - 1 - ci) if rev else ci
        r0 = pl.multiple_of(cc * c, c)
        for h in range(HG_HEADS):
            cols = slice(h * LANES, (h + 1) * LANES)
            hq = hq_ref[pl.ds(r0, c), cols].astype(F32)
            hf = hf_ref[pl.ds(r0, c), cols].astype(F32)
            v = hi_ref[pl.ds(r0, c), cols]
            lb = lb_all[:, cols]
            q = _silu(hq) * q_scale
            f = lb + (1.0 - lb) * _sigmoid(hf)
            k = 1.0 - f
            b = _chunk_cumsum(jnp.log(f), row, rev)
            st = st_scr[h]
            o = lax.dot_general((q * jnp.exp(b)).astype(BF16), st.astype(BF16),
                                (((1,), (1,)), ((), ())), preferred_element_type=F32)
            a = jnp.zeros((c, c), F32)
            m = c // 2
            while m >= 1:
                e = jnp.exp(-jnp.abs(b - _level_ref(b, m, rev)))
                upper = (row & m) != 0
                q_rows = (~upper) if rev else upper
                qe = jnp.where(q_rows, q * e, 0.0).astype(BF16)
                ke = jnp.where(q_rows, 0.0, k * e).astype(BF16)
                al = lax.dot_general(qe, ke, (((1,), (1,)), ((), ())), preferred_element_type=F32)
                a = a + jnp.where(diff_bits < 2 * m, al, 0.0)
                m //= 2
            diag = jnp.sum(q * k, axis=-1, keepdims=True)
            o = o + jnp.dot(a.astype(BF16), v, preferred_element_type=F32) + diag * v.astype(F32)
            o_ref[pl.ds(r0, c), cols] = o.astype(o_ref.dtype)
            b_end = b[0:1, :] if rev else b[c - 1:c, :]
            k_end = (k * jnp.exp(b_end - b)).astype(BF16)
            upd = lax.dot_general(v, k_end, (((0,), (0,)), ((), ())), preferred_element_type=F32)
            st_scr[h] = st * jnp.exp(b_end) + upd
        return carry

    lax.fori_loop(0, n_chunks, body, 0)


def _hgrn(proj, lb_logits_dir, batch, seq, tt, rev, layer):
    n = proj.shape[0]
    nt = seq // tt
    col_f = (COL_HFB if rev else COL_HFF) // HG_F

    def rows(b, i):
        return b * nt + ((nt - 1 - i) if rev else i)

    return pl.pallas_call(
        functools.partial(_hgrn_kernel, rev=rev, layer=layer, chunk=min(HG_CHUNK, tt)),
        grid=(batch, nt),
        in_specs=[
            pl.BlockSpec((tt, HG_F), lambda b, i: (rows(b, i), COL_HQ // HG_F)),
            pl.BlockSpec((tt, HG_F), lambda b, i: (rows(b, i), col_f)),
            pl.BlockSpec((tt, HG_F), lambda b, i: (rows(b, i), COL_HI // HG_F)),
            pl.BlockSpec((DEPTH, HG_F), lambda b, i: (0, 0)),
        ],
        out_specs=pl.BlockSpec((tt, HG_F), lambda b, i: (rows(b, i), 0)),
        out_shape=jax.ShapeDtypeStruct((n, HG_F), BF16),
        scratch_shapes=[pltpu.VMEM((HG_HEADS, HG_DV, HG_DK), F32)],
        compiler_params=_params(("parallel", "arbitrary")),
        name="hgrn2_bwd" if rev else "hgrn2_fwd",
    )(proj, proj, proj, lb_logits_dir)


def _mixout_kernel(att_ref, of_ref, ob_ref, hg_ref, ga_ref, gb_ref, x_ref, gt_ref, ghg_ref,
                   gn_ref, wa_ref, wb_ref, wo_ref, o_ref):
    ghg = ghg_ref[...]
    recs = []
    for h in range(HG_HEADS):
        cols = slice(h * LANES, (h + 1) * LANES)
        o = of_ref[:, cols].astype(F32) + ob_ref[:, cols].astype(F32)
        recs.append((_rms(o, ghg) * _silu(hg_ref[:, cols].astype(F32))).astype(BF16))
    rec = jnp.concatenate(recs, axis=1)
    pa = jnp.dot(att_ref[...], wa_ref[...], preferred_element_type=F32)
    pb = jnp.dot(rec, wb_ref[...], preferred_element_type=F32)
    merged = (_sigmoid(ga_ref[...].astype(F32)) * pa + _sigmoid(gb_ref[...].astype(F32)) * pb)
    y = jnp.dot(merged.astype(BF16), wo_ref[...], preferred_element_type=F32)
    o_ref[...] = x_ref[...] + gt_ref[0] * _rms(y, gn_ref[...])


def _mixout(att, o_f, o_b, proj, x, gt, g_hgrn, g_norm, w_a, w_b, w_o, seq, tm):
    n = x.shape[0]
    per_b = seq // tm
    half = pl.BlockSpec((tm, ATT_Q), lambda i: (i, 0))
    const = lambda shape: pl.BlockSpec(shape, lambda i: (0,) * len(shape))
    return pl.pallas_call(
        _mixout_kernel,
        grid=(n // tm,),
        in_specs=[
            half, half, half,
            pl.BlockSpec((tm, HG_F), lambda i: (i, COL_HG // HG_F)),
            pl.BlockSpec((tm, D_MODEL), lambda i: (i, COL_GA // D_MODEL)),
            pl.BlockSpec((tm, D_MODEL), lambda i: (i, COL_GB // D_MODEL)),
            pl.BlockSpec((tm, D_MODEL), lambda i: (i, 0)),
            pl.BlockSpec((1, 1, D_MODEL), lambda i: (i // per_b, 0, 0)),
            const((1, LANES)), const((1, D_MODEL)),
            const((ATT_Q, D_MODEL)), const((HG_F, D_MODEL)), const((D_MODEL, D_MODEL)),
        ],
        out_specs=pl.BlockSpec((tm, D_MODEL), lambda i: (i, 0)),
        out_shape=jax.ShapeDtypeStruct((n, D_MODEL), F32),
        compiler_params=_params(("parallel",)),
        name="mixer_out",
    )(att, o_f, o_b, proj, proj, proj, x, gt, g_hgrn, g_norm, w_a, w_b, w_o)


def _router_kernel(x_ref, g_ref, sc_ref, sh_ref, wr_ref, br_ref, h_ref, sel_ref):
    h = _rms(x_ref[...], g_ref[...]) * (1.0 + sc_ref[0]) + sh_ref[0]
    h_ref[...] = h
    logits = jnp.dot(h, wr_ref[...], preferred_element_type=F32,
                     precision=lax.Precision.HIGHEST) + br_ref[...]
    lane = lax.broadcasted_iota(jnp.int32, logits.shape, 1).astype(F32)
    logits = jnp.where(lane < N_EXPERTS, logits, -jnp.inf)
    m1 = jnp.max(logits, axis=-1, keepdims=True)
    i1 = jnp.min(jnp.where(logits == m1, lane, float(LANES)), axis=-1, keepdims=True)
    rest = jnp.where(lane == i1, -jnp.inf, logits)
    m2 = jnp.max(rest, axis=-1, keepdims=True)
    i2 = jnp.min(jnp.where(rest == m2, lane, float(LANES)), axis=-1, keepdims=True)
    e2 = jnp.exp(m2 - m1)
    w1 = 1.0 / (1.0 + e2)
    w2 = e2 / (1.0 + e2)
    sel_ref[...] = (jnp.where(lane == SEL_E1, i1, 0.0) + jnp.where(lane == SEL_E2, i2, 0.0)
                    + jnp.where(lane == SEL_W1, w1, 0.0) + jnp.where(lane == SEL_W2, w2, 0.0))


def _router(x, g, sc, sh, w_r, b_r, seq, tm):
    n = x.shape[0]
    per_b = seq // tm
    return pl.pallas_call(
        _router_kernel,
        grid=(n // tm,),
        in_specs=[
            pl.BlockSpec((tm, D_MODEL), lambda i: (i, 0)),
            pl.BlockSpec((1, D_MODEL), lambda i: (0, 0)),
            pl.BlockSpec((1, 1, D_MODEL), lambda i: (i // per_b, 0, 0)),
            pl.BlockSpec((1, 1, D_MODEL), lambda i: (i // per_b, 0, 0)),
            pl.BlockSpec((D_MODEL, LANES), lambda i: (0, 0)),
            pl.BlockSpec((1, LANES), lambda i: (0, 0)),
        ],
        out_specs=[pl.BlockSpec((tm, D_MODEL), lambda i: (i, 0)),
                   pl.BlockSpec((tm, LANES), lambda i: (i, 0))],
        out_shape=[jax.ShapeDtypeStruct((n, D_MODEL), F32),
                   jax.ShapeDtypeStruct((n, LANES), F32)],
        compiler_params=_params(("parallel",)),
        name="router_top2",
    )(x, g, sc, sh, w_r, b_r)


def _route_plan(sel, tm, tmc):
    n = sel.shape[0]
    r = 2 * n
    flat_e = sel[:, :2].astype(jnp.int32).reshape(-1)
    onehot = (flat_e[:, None] == jnp.arange(N_EXPERTS, dtype=jnp.int32)[None, :]).astype(jnp.int32)
    csum = jnp.cumsum(onehot, axis=0)
    rank = jnp.sum(csum * onehot, axis=1) - 1
    counts = csum[-1]
    padded = (counts + tm - 1) // tm * tm
    ends = jnp.cumsum(padded)
    pos = (ends - padded)[flat_e] + rank
    n_tiles = r // tm + N_EXPERTS
    src = jnp.zeros((n_tiles * tm,), jnp.int32).at[pos].set(jnp.arange(r, dtype=jnp.int32) // 2)
    n_valid = (ends[-1] // tm).astype(jnp.int32)
    tile_start = jnp.arange(n_tiles, dtype=jnp.int32) * tm
    texp = jnp.minimum(jnp.searchsorted(ends, tile_start, side="right"), N_EXPERTS - 1)
    texp = jnp.where(tile_start < ends[-1], texp, texp[jnp.maximum(n_valid - 1, 0)])
    pos_tiles = pos.reshape(n // tmc, tmc, 2).transpose(0, 2, 1).reshape(n // tmc, 2 * tmc)
    return src.reshape(n_tiles, tm), texp.astype(jnp.int32), n_valid.reshape(1), pos_tiles


def _moe_kernel(texp_ref, nv_ref, src_hbm, h_hbm, wa_ref, wb_ref, wo_ref, y_ref,
                hbuf, hb16, idx_smem, gsem, isem, *, tm):
    t = pl.program_id(0)
    f = pl.program_id(1)
    nv = nv_ref[0]
    slot = t % 2
    other = 1 - slot

    def idx_copy(tile, s):
        return pltpu.make_async_copy(src_hbm.at[pl.ds(tile, 1)], idx_smem.at[pl.ds(s, 1)], isem.at[s])

    def issue_rows(s, lo, hi):
        def body(r, carry):
            tok = idx_smem[s, r]
            pltpu.make_async_copy(h_hbm.at[pl.ds(tok, 1)], hbuf.at[s, pl.ds(r, 1)], gsem.at[s]).start()
            return carry
        lax.fori_loop(lo, hi, body, 0, unroll=8)

    def wait_rows(s):
        pltpu.make_async_copy(h_hbm.at[pl.ds(0, tm)], hbuf.at[s], gsem.at[s]).wait()

    @pl.when((t == 0) & (f == 0) & (nv > 0))
    def _():
        first = idx_copy(0, 0)
        first.start()
        first.wait()
        issue_rows(0, 0, tm)

    @pl.when((f == 0) & (t < nv))
    def _():
        wait_rows(slot)
        hb16[...] = hbuf[slot].astype(BF16)

        @pl.when(t + 1 < nv)
        def _():
            idx_copy(t + 1, other).start()

    @pl.when((f == 1) & (t + 1 < nv))
    def _():
        idx_copy(t + 1, other).wait()
        issue_rows(other, 0, tm // 2)

    @pl.when((f == 2) & (t + 1 < nv))
    def _():
        issue_rows(other, tm // 2, tm)

    @pl.when(t < nv)
    def _():
        h = hb16[...]
        a = jnp.dot(h, wa_ref[0], preferred_element_type=F32)
        b = jnp.dot(h, wb_ref[0], preferred_element_type=F32)
        y = jnp.dot((_silu(a) * b).astype(BF16), wo_ref[0], preferred_element_type=F32)

        @pl.when(f == 0)
        def _():
            y_ref[...] = y

        @pl.when(f > 0)
        def _():
            y_ref[...] += y

    @pl.when((t >= nv) & (f == 0))
    def _():
        y_ref[...] = jnp.zeros(y_ref.shape, F32)


def _moe_experts(h, src, texp, n_valid, w_in, w_out, tm, chunk):
    n_tiles = src.shape[0]
    ff = w_in.shape[2] // 2
    n_fc = ff // chunk
    assert n_fc >= 3, "row gathers for the next tile are issued during chunks 1 and 2"

    def fidx(t, f, nv):
        return jnp.where(t < nv[0], f, n_fc - 1)

    grid_spec = pltpu.PrefetchScalarGridSpec(
        num_scalar_prefetch=2,
        grid=(n_tiles, n_fc),
        in_specs=[
            pl.BlockSpec(memory_space=pl.ANY),
            pl.BlockSpec(memory_space=pl.ANY),
            pl.BlockSpec((1, D_MODEL, chunk), lambda t, f, te, nv: (te[t], 0, fidx(t, f, nv))),
            pl.BlockSpec((1, D_MODEL, chunk), lambda t, f, te, nv: (te[t], 0, n_fc + fidx(t, f, nv))),
            pl.BlockSpec((1, chunk, D_MODEL), lambda t, f, te, nv: (te[t], fidx(t, f, nv), 0)),
        ],
        out_specs=pl.BlockSpec((tm, D_MODEL), lambda t, f, te, nv: (t, 0)),
        scratch_shapes=[
            pltpu.VMEM((2, tm, D_MODEL), F32),
            pltpu.VMEM((tm, D_MODEL), BF16),
            pltpu.SMEM((2, tm), jnp.int32),
            pltpu.SemaphoreType.DMA((2,)),
            pltpu.SemaphoreType.DMA((2,)),
        ],
    )
    return pl.pallas_call(
        functools.partial(_moe_kernel, tm=tm),
        grid_spec=grid_spec,
        out_shape=jax.ShapeDtypeStruct((n_tiles * tm, D_MODEL), F32),
        compiler_params=_params(("arbitrary", "arbitrary")),
        name="moe_experts",
    )(texp, n_valid, src, h, w_in, w_in, w_out)


def _combine_kernel(pos_hbm, y_hbm, x_ref, sel_ref, gt_ref, gn_ref, o_ref, ybuf, idx_smem, gsem, isem,
                    *, tmc):
    i = pl.program_id(0)
    fetch = pltpu.make_async_copy(pos_hbm.at[pl.ds(i, 1)], idx_smem, isem)
    fetch.start()
    fetch.wait()

    def body(r, carry):
        row = idx_smem[0, r]
        pltpu.make_async_copy(y_hbm.at[pl.ds(row, 1)], ybuf.at[pl.ds(r, 1)], gsem).start()
        return carry
    lax.fori_loop(0, 2 * tmc, body, 0, unroll=8)
    pltpu.make_async_copy(y_hbm.at[pl.ds(0, 2 * tmc)], ybuf, gsem).wait()

    sel = sel_ref[...]
    w1 = sel[:, SEL_W1:SEL_W1 + 1]
    w2 = sel[:, SEL_W2:SEL_W2 + 1]
    y = w1 * ybuf[0:tmc, :] + w2 * ybuf[tmc:2 * tmc, :]
    o_ref[...] = x_ref[...] + gt_ref[0] * _rms(y, gn_ref[...])


def _combine(pos_tiles, y_sorted, x, sel, gt, gn, seq, tmc):
    n = x.shape[0]
    per_b = seq // tmc
    return pl.pallas_call(
        functools.partial(_combine_kernel, tmc=tmc),
        grid=(n // tmc,),
        in_specs=[
            pl.BlockSpec(memory_space=pl.ANY),
            pl.BlockSpec(memory_space=pl.ANY),
            pl.BlockSpec((tmc, D_MODEL), lambda i: (i, 0)),
            pl.BlockSpec((tmc, LANES), lambda i: (i, 0)),
            pl.BlockSpec((1, 1, D_MODEL), lambda i: (i // per_b, 0, 0)),
            pl.BlockSpec((1, D_MODEL), lambda i: (0, 0)),
        ],
        out_specs=pl.BlockSpec((tmc, D_MODEL), lambda i: (i, 0)),
        out_shape=jax.ShapeDtypeStruct((n, D_MODEL), F32),
        scratch_shapes=[
            pltpu.VMEM((2 * tmc, D_MODEL), F32),
            pltpu.SMEM((1, 2 * tmc), jnp.int32),
            pltpu.SemaphoreType.DMA,
            pltpu.SemaphoreType.DMA,
        ],
        compiler_params=_params(("arbitrary",)),
        name="moe_combine",
    )(pos_tiles, y_sorted, x, sel, gt, gn)


def _ffn_kernel(x_ref, g_ref, sc_ref, sh_ref, gt_ref, gn_ref, wa_ref, wb_ref, wo_ref,
                o_ref, h_scr, acc_scr):
    f = pl.program_id(1)

    @pl.when(f == 0)
    def _():
        y = _rms(x_ref[...], g_ref[...])
        h_scr[...] = (y * (1.0 + sc_ref[0]) + sh_ref[0]).astype(BF16)
        acc_scr[...] = jnp.zeros(acc_scr.shape, F32)

    h = h_scr[...]
    a = jnp.dot(h, wa_ref[...], preferred_element_type=F32)
    b = jnp.dot(h, wb_ref[...], preferred_element_type=F32)
    acc_scr[...] += jnp.dot((_silu(a) * b).astype(BF16), wo_ref[...], preferred_element_type=F32)

    @pl.when(f == pl.num_programs(1) - 1)
    def _():
        o_ref[...] = x_ref[...] + gt_ref[0] * _rms(acc_scr[...], gn_ref[...])


def _ffn(x, g, sc, sh, gt, gn, w_in, w_out, seq, tm, chunk):
    n = x.shape[0]
    per_b = seq // tm
    n_fc = w_in.shape[1] // 2 // chunk
    vec = pl.BlockSpec((1, D_MODEL), lambda i, f: (0, 0))
    mod = pl.BlockSpec((1, 1, D_MODEL), lambda i, f: (i // per_b, 0, 0))
    return pl.pallas_call(
        _ffn_kernel,
        grid=(n // tm, n_fc),
        in_specs=[
            pl.BlockSpec((tm, D_MODEL), lambda i, f: (i, 0)), vec, mod, mod, mod, vec,
            pl.BlockSpec((D_MODEL, chunk), lambda i, f: (0, f)),
            pl.BlockSpec((D_MODEL, chunk), lambda i, f: (0, n_fc + f)),
            pl.BlockSpec((chunk, D_MODEL), lambda i, f: (f, 0)),
        ],
        out_specs=pl.BlockSpec((tm, D_MODEL), lambda i, f: (i, 0)),
        out_shape=jax.ShapeDtypeStruct((n, D_MODEL), F32),
        scratch_shapes=[pltpu.VMEM((tm, D_MODEL), BF16), pltpu.VMEM((tm, D_MODEL), F32)],
        compiler_params=_params(("parallel", "arbitrary")),
        name="dense_ffn",
    )(x, g, sc, sh, gt, gn, w_in, w_in, w_out)


def _rope_tables(seq):
    t = jnp.arange(seq, dtype=jnp.int32)
    row = (t // GRID_W).astype(F32)
    col = (t % GRID_W).astype(F32)
    inv = ROPE_THETA ** (-jnp.arange(ROPE_F, dtype=F32) / ROPE_F)
    ang_r = row[:, None] * inv
    ang_c = col[:, None] * inv
    cos = jnp.concatenate([jnp.cos(ang_r)] * 2 + [jnp.cos(ang_c)] * 2, axis=1)
    sin = jnp.concatenate([-jnp.sin(ang_r), jnp.sin(ang_r), -jnp.sin(ang_c), jnp.sin(ang_c)], axis=1)
    return jnp.tile(cos, (1, 2)), jnp.tile(sin, (1, 2))


def _prep_weights(w_in, g_qk, w_branch, w_o, w_ffn_in, w_ffn_out, w_router, b_router,
                  w_moe_in, w_moe_out):
    o_aq, o_ak, o_av, o_hq = 0, ATT_Q, ATT_Q + ATT_KV, ATT_Q + 2 * ATT_KV
    o_ga = o_hq + 5 * HG_F
    w_in_p = jnp.concatenate(
        [w_in[:, :, o_ga:], w_in[:, :, o_hq:o_ga], w_in[:, :, o_aq:o_hq]], axis=2).astype(BF16)
    pad_c = D_FF_PAD - D_FF
    w_fa = jnp.pad(w_ffn_in[:, :, :D_FF], ((0, 0), (0, 0), (0, pad_c)))
    w_fb = jnp.pad(w_ffn_in[:, :, D_FF:], ((0, 0), (0, 0), (0, pad_c)))
    w_ffn_in_p = jnp.concatenate([w_fa, w_fb], axis=2).astype(BF16)
    w_ffn_out_p = jnp.pad(w_ffn_out, ((0, 0), (0, pad_c), (0, 0))).astype(BF16)
    w_r_p = jnp.pad(w_router, ((0, 0), (0, 0), (0, LANES - N_EXPERTS)))
    b_r_p = jnp.pad(b_router, ((0, 0), (0, LANES - N_EXPERTS)))[:, None, :]
    g_q = jnp.tile(g_qk[:, 0, :], (1, LANES // HEAD_DIM))[:, None, :]
    g_k = jnp.tile(g_qk[:, 1, :], (1, LANES // HEAD_DIM))[:, None, :]
    return dict(w_in=w_in_p, w_branch=w_branch.astype(BF16), w_o=w_o.astype(BF16),
                w_ffn_in=w_ffn_in_p, w_ffn_out=w_ffn_out_p, w_r=w_r_p, b_r=b_r_p,
                w_moe_in=w_moe_in.astype(BF16), w_moe_out=w_moe_out.astype(BF16),
                g_q=g_q, g_k=g_k)


def _tile(seq, want):
    return min(seq, want)


def _trunk(x3, mods, g_norm, lb_logits, g_hgrn, wts):
    batch, seq, _ = x3.shape
    x = x3.reshape(batch * seq, D_MODEL)
    cos_t, sin_t = _rope_tables(seq)
    idx = np.arange(LANES)
    bd = jnp.asarray((idx[:, None] // HEAD_DIM == idx[None, :] // HEAD_DIM) / HEAD_DIM, BF16)
    tm_big = _tile(seq, 1024)
    tm = _tile(seq, 512)
    for l in range(DEPTH):
        mod = mods[l]
        sh1, sc1, gt1, sh2, sc2, gt2 = [m[:, None, :] for m in jnp.split(mod, 6, axis=-1)]
        gn = g_norm[l][:, None, :]
        proj = _inproj(x, gn[0], sc1, sh1, wts["w_in"][l], seq, tm_big)
        qh, ka, kb, vta, vtb = _rope(proj, cos_t, sin_t, wts["g_q"][l], wts["g_k"][l], bd, seq, tm)
        att = _attention(qh, ka, kb, vta, vtb, batch, seq, tm, tm)
        o_f = _hgrn(proj, lb_logits[0], batch, seq, tm, False, l)
        o_b = _hgrn(proj, lb_logits[1], batch, seq, tm, True, l)
        x = _mixout(att, o_f, o_b, proj, x, gt1, g_hgrn[l][None, :], gn[1],
                    wts["w_branch"][l, 0], wts["w_branch"][l, 1], wts["w_o"][l], seq, tm)
        if l % 2 == 0:
            x = _ffn(x, gn[2], sc2, sh2, gt2, gn[3], wts["w_ffn_in"][l // 2],
                     wts["w_ffn_out"][l // 2], seq, tm, FFN_CHUNK_DENSE)
        else:
            h, sel = _router(x, gn[2], sc2, sh2, wts["w_r"][l // 2], wts["b_r"][l // 2], seq, tm)
            moe_tile = min(MOE_TILE, 2 * batch * seq)
            src, texp, n_valid, pos_tiles = _route_plan(sel, moe_tile, tm)
            y_sorted = _moe_experts(h, src, texp, n_valid, wts["w_moe_in"][l // 2],
                                    wts["w_moe_out"][l // 2], moe_tile, FFN_CHUNK_EXPERT)
            x = _combine(pos_tiles, y_sorted, x, sel, gt2, gn[3], seq, tm)
    return x.reshape(batch, seq, D_MODEL)


def kernel(x_prompt, x_sample, c_prompt, c_sample, w_ada, b_ada, g_norm, w_in, g_qk, lb_logits,
           g_hgrn, w_branch, w_o, w_ffn_in, w_ffn_out, w_router, b_router, w_moe_in, w_moe_out):
    wts = _prep_weights(w_in, g_qk, w_branch, w_o, w_ffn_in, w_ffn_out, w_router, b_router,
                        w_moe_in, w_moe_out)
    n_p, n_s = c_prompt.shape[0], c_sample.shape[0]
    c_all = jnp.concatenate([c_prompt, c_sample], axis=0)
    pad = (-c_all.shape[0]) % SUBLANES
    c_all = jnp.pad(c_all, ((0, pad), (0, 0)))
    mods = _ada(c_all, w_ada, b_ada)
    y_p = _trunk(x_prompt, mods[:, :n_p], g_norm, lb_logits, g_hgrn, wts)
    y_s = _trunk(x_sample, mods[:, n_p:n_p + n_s], g_norm, lb_logits, g_hgrn, wts)
    return (y_p, y_s)
```

```python
import functools

import jax
import jax.numpy as jnp
import numpy as np
from jax import lax
from jax.experimental import pallas as pl
from jax.experimental.pallas import tpu as pltpu

F32 = jnp.float32
BF16 = jnp.bfloat16

D_MODEL = 1024
DEPTH = 4
GRID_W = 64
N_HEADS = 8
N_KV_HEADS = 2
HEAD_DIM = 64
KV_GROUP = N_HEADS // N_KV_HEADS
ROPE_F = HEAD_DIM // 4
ROPE_THETA = 10000.0
HG_HEADS = 4
HG_DK = 128
HG_DV = 128
ATT_Q = N_HEADS * HEAD_DIM
ATT_KV = N_KV_HEADS * HEAD_DIM
HG_F = HG_HEADS * HG_DK
D_FF = 2752
N_EXPERTS = 8
D_FF_EXPERT = 3584
EPS = 1e-6

LANES = 128
SUBLANES = 8
VMEM_LIMIT_BYTES = 48 * 1024 * 1024

COL_GA, COL_GB = 0, 1024
COL_HQ, COL_HFF, COL_HFB, COL_HI, COL_HG = 2048, 2560, 3072, 3584, 4096
COL_AQ, COL_AK, COL_AV = 4608, 5120, 5248
IN_COLS = 5376
IN_TILE_N = 768

D_FF_PAD = 2816
FFN_CHUNK_DENSE = 1408
FFN_CHUNK_EXPERT = 896
HG_CHUNK = 128
MOE_TILE = 512
SEL_E1, SEL_E2, SEL_W1, SEL_W2 = 0, 1, 2, 3


def _params(sem):
    return pltpu.CompilerParams(dimension_semantics=sem, vmem_limit_bytes=VMEM_LIMIT_BYTES)


def _sigmoid(x):
    return 1.0 / (1.0 + jnp.exp(-x))


def _silu(x):
    return x * _sigmoid(x)


def _rms(x, g):
    ms = jnp.mean(x * x, axis=-1, keepdims=True)
    return x * lax.rsqrt(ms + EPS) * g


def _ada_kernel(c_ref, w_ref, b_ref, o_ref):
    c = c_ref[...]
    o_ref[0] = jnp.dot(_silu(c), w_ref[0], preferred_element_type=F32,
                       precision=lax.Precision.HIGHEST) + b_ref[0]


def _ada(c, w_ada, b_ada):
    bp = c.shape[0]
    n_col = 6 * D_MODEL // D_MODEL
    return pl.pallas_call(
        _ada_kernel,
        grid=(DEPTH, n_col),
        in_specs=[
            pl.BlockSpec((bp, D_MODEL), lambda l, j: (0, 0)),
            pl.BlockSpec((1, D_MODEL, D_MODEL), lambda l, j: (l, 0, j)),
            pl.BlockSpec((1, 1, D_MODEL), lambda l, j: (l, 0, j)),
        ],
        out_specs=pl.BlockSpec((1, bp, D_MODEL), lambda l, j: (l, 0, j)),
        out_shape=jax.ShapeDtypeStruct((DEPTH, bp, 6 * D_MODEL), F32),
        compiler_params=_params(("arbitrary", "arbitrary")),
        name="ada_mod",
    )(c, w_ada, b_ada.reshape(DEPTH, 1, 6 * D_MODEL))


def _inproj_kernel(x_ref, g_ref, sc_ref, sh_ref, w_ref, o_ref, h_scr):
    @pl.when(pl.program_id(1) == 0)
    def _():
        y = _rms(x_ref[...], g_ref[...])
        h_scr[...] = (y * (1.0 + sc_ref[0]) + sh_ref[0]).astype(BF16)

    o_ref[...] = jnp.dot(h_scr[...], w_ref[...], preferred_element_type=F32).astype(BF16)


def _inproj(x, g, sc, sh, w, seq, tm):
    n = x.shape[0]
    per_b = seq // tm
    return pl.pallas_call(
        _inproj_kernel,
        grid=(n // tm, IN_COLS // IN_TILE_N),
        in_specs=[
            pl.BlockSpec((tm, D_MODEL), lambda i, j: (i, 0)),
            pl.BlockSpec((1, D_MODEL), lambda i, j: (0, 0)),
            pl.BlockSpec((1, 1, D_MODEL), lambda i, j: (i // per_b, 0, 0)),
            pl.BlockSpec((1, 1, D_MODEL), lambda i, j: (i // per_b, 0, 0)),
            pl.BlockSpec((D_MODEL, IN_TILE_N), lambda i, j: (0, j)),
        ],
        out_specs=pl.BlockSpec((tm, IN_TILE_N), lambda i, j: (i, j)),
        out_shape=jax.ShapeDtypeStruct((n, IN_COLS), BF16),
        scratch_shapes=[pltpu.VMEM((tm, D_MODEL), BF16)],
        compiler_params=_params(("parallel", "arbitrary")),
        name="in_proj",
    )(x, g, sc, sh, w)


def _rope_kernel(aq_ref, ak_ref, av_ref, cos_ref, sin_ref, gq_ref, gk_ref, bd_ref,
                 qh_ref, ka_ref, kb_ref, vt_ref):
    tm = aq_ref.shape[0]
    lane = lax.broadcasted_iota(jnp.int32, (tm, LANES), 1)
    first_half = (lane % (2 * ROPE_F)) < ROPE_F
    low = lane < HEAD_DIM
    cos = cos_ref[...]
    sin = sin_ref[...]
    bd = bd_ref[...]

    def norm_rope(x, g):
        xx = x * x
        hi = xx.astype(BF16)
        lo = (xx - hi.astype(F32)).astype(BF16)
        ms = (jnp.dot(hi, bd, preferred_element_type=F32)
              + jnp.dot(lo, bd, preferred_element_type=F32))
        y = x * lax.rsqrt(ms + EPS) * g
        nxt = pltpu.roll(y, LANES - ROPE_F, 1)
        prv = pltpu.roll(y, ROPE_F, 1)
        return y * cos + jnp.where(first_half, nxt, prv) * sin

    gq = gq_ref[...]
    scale = HEAD_DIM ** -0.5 * float(np.log2(np.e))
    for j in range(ATT_Q // LANES):
        r = norm_rope(aq_ref[:, j * LANES:(j + 1) * LANES].astype(F32), gq) * scale
        qh_ref[:, (2 * j) * LANES:(2 * j + 1) * LANES] = jnp.where(low, r, 0.0).astype(BF16)
        qh_ref[:, (2 * j + 1) * LANES:(2 * j + 2) * LANES] = jnp.where(low, 0.0, r).astype(BF16)

    k = norm_rope(ak_ref[...].astype(F32), gk_ref[...])
    ka_ref[...] = k.astype(BF16)
    kb_ref[...] = pltpu.roll(k, HEAD_DIM, 1).astype(BF16)
    v = av_ref[...].astype(F32)
    vt = v.T
    vt_sw = pltpu.roll(v, HEAD_DIM, 1).T
    top = lax.broadcasted_iota(jnp.int32, vt.shape, 0) < HEAD_DIM
    vt_ref[0, 0] = jnp.where(top, vt, 1.0).astype(BF16)
    vt_ref[0, 1] = jnp.where(top, 1.0, vt_sw).astype(BF16)
    vt_ref[1, 0] = jnp.where(top, vt_sw, 1.0).astype(BF16)
    vt_ref[1, 1] = jnp.where(top, 1.0, vt).astype(BF16)


def _rope(proj, cos_t, sin_t, gq, gk, bd, seq, tm):
    n = proj.shape[0]
    per_b = seq // tm
    kv_spec = pl.BlockSpec((tm, LANES), lambda i: (i, 0))
    vt_spec = pl.BlockSpec((N_KV_HEADS, 2, LANES, tm), lambda i: (0, 0, 0, i))
    return pl.pallas_call(
        _rope_kernel,
        grid=(n // tm,),
        in_specs=[
            pl.BlockSpec((tm, ATT_Q), lambda i: (i, COL_AQ // ATT_Q)),
            pl.BlockSpec((tm, LANES), lambda i: (i, COL_AK // LANES)),
            pl.BlockSpec((tm, LANES), lambda i: (i, COL_AV // LANES)),
            pl.BlockSpec((tm, LANES), lambda i: (i % per_b, 0)),
            pl.BlockSpec((tm, LANES), lambda i: (i % per_b, 0)),
            pl.BlockSpec((1, LANES), lambda i: (0, 0)),
            pl.BlockSpec((1, LANES), lambda i: (0, 0)),
            pl.BlockSpec((LANES, LANES), lambda i: (0, 0)),
        ],
        out_specs=[pl.BlockSpec((tm, N_HEADS * LANES), lambda i: (i, 0)),
                   kv_spec, kv_spec, vt_spec],
        out_shape=[jax.ShapeDtypeStruct((n, N_HEADS * LANES), BF16)]
        + [jax.ShapeDtypeStruct((n, LANES), BF16)] * 2
        + [jax.ShapeDtypeStruct((N_KV_HEADS, 2, LANES, n), BF16)],
        compiler_params=_params(("parallel",)),
        name="qk_norm_rope",
    )(proj, proj, proj, cos_t, sin_t, gq, gk, bd)


def _attn_kernel(q_ref, ka_ref, kb_ref, vt_ref, o_ref, m_scr, acc_scr):
    j = pl.program_id(2)

    @pl.when(j == 0)
    def _():
        m_scr[...] = jnp.full(m_scr.shape, -jnp.inf, F32)
        acc_scr[...] = jnp.zeros(acc_scr.shape, F32)

    def scores(h):
        g, half = h // KV_GROUP, h % 2
        k = (ka_ref if g == half else kb_ref)[...]
        q = q_ref[:, h * LANES:(h + 1) * LANES]
        return lax.dot_general(k, q, (((1,), (1,)), ((), ())), preferred_element_type=F32)

    st_next = scores(0)
    for h in range(N_HEADS):
        g, half = h // KV_GROUP, h % 2
        vt = vt_ref[g, half]
        st = st_next
        if h + 1 < N_HEADS:
            st_next = scores(h + 1)
        m_prev = m_scr[h]
        m_new = jnp.maximum(m_prev, jnp.max(st, axis=0, keepdims=True))
        alpha = jnp.exp2(m_prev - m_new)
        pt = jnp.exp2((st - m_new).astype(BF16))
        acc_scr[h] = alpha * acc_scr[h] + jnp.dot(vt, pt, preferred_element_type=F32)
        m_scr[h] = m_new

    @pl.when(j == pl.num_programs(2) - 1)
    def _():
        tq = q_ref.shape[0]
        low = lax.broadcasted_iota(jnp.int32, (tq, LANES), 1) < HEAD_DIM
        for jj in range(N_HEADS // 2):
            even = acc_scr[2 * jj]
            odd = acc_scr[2 * jj + 1]
            even = (even / even[HEAD_DIM:HEAD_DIM + 1, :]).T
            odd = (odd / odd[0:1, :]).T
            o_ref[:, jj * LANES:(jj + 1) * LANES] = jnp.where(low, even, odd).astype(BF16)


def _attention(qh, ka, kb, vt, batch, seq, tq, tk):
    n = qh.shape[0]
    nq, nk = seq // tq, seq // tk
    k_spec = pl.BlockSpec((tk, LANES), lambda b, i, j: (b * nk + j, 0))
    vt_spec = pl.BlockSpec((N_KV_HEADS, 2, LANES, tk), lambda b, i, j: (0, 0, 0, b * nk + j))
    return pl.pallas_call(
        _attn_kernel,
        grid=(batch, nq, nk),
        in_specs=[pl.BlockSpec((tq, N_HEADS * LANES), lambda b, i, j: (b * nq + i, 0)),
                  k_spec, k_spec, vt_spec],
        out_specs=pl.BlockSpec((tq, ATT_Q), lambda b, i, j: (b * nq + i, 0)),
        out_shape=jax.ShapeDtypeStruct((n, ATT_Q), BF16),
        scratch_shapes=[pltpu.VMEM((N_HEADS, 1, tq), F32),
                        pltpu.VMEM((N_HEADS, LANES, tq), F32)],
        compiler_params=_params(("parallel", "parallel", "arbitrary")),
        name="gqa_attention",
    )(qh, ka, kb, vt)


def _level_ref(b, m, rev):
    c = b.shape[0]
    g = 2 * m
    if g >= SUBLANES:
        b3 = b.reshape(c // g, g, LANES)
        r = m if rev else m - 1
        return jnp.broadcast_to(b3[:, r:r + 1, :], b3.shape).reshape(c, LANES)
    b3 = b.reshape(c // SUBLANES, SUBLANES, LANES)
    sub = lax.broadcasted_iota(jnp.int32, b3.shape, 1)
    out = None
    for p0 in range(0, SUBLANES, g):
        r = p0 + (m if rev else m - 1)
        cand = jnp.broadcast_to(b3[:, r:r + 1, :], b3.shape)
        out = cand if out is None else jnp.where(sub >= p0, cand, out)
    return out.reshape(c, LANES)


def _chunk_cumsum(x, tri):
    hi = x.astype(BF16)
    r1 = x - hi.astype(F32)
    mid = r1.astype(BF16)
    lo = (r1 - mid.astype(F32)).astype(BF16)
    return (jnp.dot(tri, hi, preferred_element_type=F32)
            + jnp.dot(tri, mid, preferred_element_type=F32)
            + jnp.dot(tri, lo, preferred_element_type=F32))


def _neg_abs(z):
    bits = lax.bitcast_convert_type(z, jnp.int32) | jnp.int32(-2 ** 31)
    return lax.bitcast_convert_type(bits, F32)


def _hgrn_kernel(hq_ref, hf_ref, hi_ref, lbl_ref, o_ref, st_scr, *, rev, layer, chunk):
    c = chunk
    n_chunks = hq_ref.shape[0] // c

    @pl.when(pl.program_id(1) == 0)
    def _():
        st_scr[...] = jnp.zeros(st_scr.shape, F32)

    lbl = lbl_ref[...]
    lbe = jnp.exp(lbl - jnp.max(lbl, axis=0, keepdims=True))
    lbp = lbe / jnp.sum(lbe, axis=0, keepdims=True)
    lb_all = jnp.zeros((1, HG_F), F32)
    for l in range(1, layer + 1):
        lb_all = lb_all + lbp[l:l + 1, :]

    row = lax.broadcasted_iota(jnp.int32, (c, LANES), 0)
    row_a = lax.broadcasted_iota(jnp.int32, (c, c), 0)
    col_a = lax.broadcasted_iota(jnp.int32, (c, c), 1)
    tri = jnp.where((row_a <= col_a) if rev else (row_a >= col_a), 1.0, 0.0).astype(BF16)
    lev = 31 - lax.clz(row_a ^ col_a)
    lev = jnp.where((row_a < col_a) if rev else (row_a > col_a), lev, -1)
    q_scale = HG_DK ** -0.5
    log2e = float(np.log2(np.e))
    nt_dims = (((1,), (1,)), ((), ()))

    def body(ci, carry):
        cc = (n_chunks - 1 - ci) if rev else ci
        r0 = pl.multiple_of(cc * c, c)
        heads = range(HG_HEADS)
        cols = [slice(h * LANES, (h + 1) * LANES) for h in heads]
        v = [hi_ref[pl.ds(r0, c), cols[h]] for h in heads]
        q = [_silu(hq_ref[pl.ds(r0, c), cols[h]].astype(F32)) * q_scale for h in heads]
        f = [lb_all[:, cols[h]] + (1.0 - lb_all[:, cols[h]])
             * _sigmoid(hf_ref[pl.ds(r0, c), cols[h]].astype(F32)) for h in heads]
        k = [1.0 - f[h] for h in heads]
        b = [_chunk_cumsum(jnp.log(f[h]) * log2e, tri) for h in heads]
        a = [jnp.zeros((c, c), F32) for h in heads]
        m = c // 2
        while m >= 1:
            upper = (row & m) != 0
            for h in heads:
                e = jnp.exp2(_neg_abs(b[h] - _level_ref(b[h], m, rev)).astype(BF16))
                w = jnp.where(upper, k[h], q[h]) if rev else jnp.where(upper, q[h], k[h])
                w = w.astype(BF16) * e
                al = lax.dot_general(w, w, nt_dims, preferred_element_type=F32)
                a[h] = jnp.where(lev == (m.bit_length() - 1), al, a[h])
            m //= 2
        for h in heads:
            st = st_scr[h]
            o = lax.dot_general((q[h] * jnp.exp2(b[h])).astype(BF16), st.astype(BF16), nt_dims,
                                preferred_element_type=F32)
            diag = jnp.sum(q[h] * k[h], axis=-1, keepdims=True)
            o = (o + jnp.dot(a[h].astype(BF16), v[h], preferred_element_type=F32)
                 + diag * v[h].astype(F32))
            o_ref[pl.ds(r0, c), cols[h]] = o.astype(o_ref.dtype)
            b_end = b[h][0:1, :] if rev else b[h][c - 1:c, :]
            k_end = (k[h] * jnp.exp2(b_end - b[h])).astype(BF16)
            upd = lax.dot_general(v[h], k_end, (((0,), (0,)), ((), ())), preferred_element_type=F32)
            st_scr[h] = st * jnp.exp2(b_end) + upd
        return carry

    lax.fori_loop(0, n_chunks, body, 0, unroll=True)


def _hgrn(proj, lb_logits_dir, batch, seq, tt, rev, layer):
    n = proj.shape[0]
    nt = seq // tt
    col_f = (COL_HFB if rev else COL_HFF) // HG_F

    def rows(b, i):
        return b * nt + ((nt - 1 - i) if rev else i)

    return pl.pallas_call(
        functools.partial(_hgrn_kernel, rev=rev, layer=layer, chunk=min(HG_CHUNK, tt)),
        grid=(batch, nt),
        in_specs=[
            pl.BlockSpec((tt, HG_F), lambda b, i: (rows(b, i), COL_HQ // HG_F)),
            pl.BlockSpec((tt, HG_F), lambda b, i: (rows(b, i), col_f)),
            pl.BlockSpec((tt, HG_F), lambda b, i: (rows(b, i), COL_HI // HG_F)),
            pl.BlockSpec((DEPTH, HG_F), lambda b, i: (0, 0)),
        ],
        out_specs=pl.BlockSpec((tt, HG_F), lambda b, i: (rows(b, i), 0)),
        out_shape=jax.ShapeDtypeStruct((n, HG_F), BF16),
        scratch_shapes=[pltpu.VMEM((HG_HEADS, HG_DV, HG_DK), F32)],
        compiler_params=_params(("parallel", "arbitrary")),
        name="hgrn2_bwd" if rev else "hgrn2_fwd",
    )(proj, proj, proj, lb_logits_dir)


def _mixout_kernel(att_ref, of_ref, ob_ref, hg_ref, ga_ref, gb_ref, x_ref, gt_ref, ghg_ref,
                   gn_ref, wa_ref, wb_ref, wo_ref, o_ref):
    ghg = ghg_ref[...]
    recs = []
    for h in range(HG_HEADS):
        cols = slice(h * LANES, (h + 1) * LANES)
        o = of_ref[:, cols].astype(F32) + ob_ref[:, cols].astype(F32)
        recs.append((_rms(o, ghg) * _silu(hg_ref[:, cols].astype(F32))).astype(BF16))
    rec = jnp.concatenate(recs, axis=1)
    pa = jnp.dot(att_ref[...], wa_ref[...], preferred_element_type=F32)
    pb = jnp.dot(rec, wb_ref[...], preferred_element_type=F32)
    merged = (_sigmoid(ga_ref[...].astype(F32)) * pa + _sigmoid(gb_ref[...].astype(F32)) * pb)
    y = jnp.dot(merged.astype(BF16), wo_ref[...], preferred_element_type=F32)
    o_ref[...] = x_ref[...] + gt_ref[0] * _rms(y, gn_ref[...])


def _mixout(att, o_f, o_b, proj, x, gt, g_hgrn, g_norm, w_a, w_b, w_o, seq, tm):
    n = x.shape[0]
    per_b = seq // tm
    half = pl.BlockSpec((tm, ATT_Q), lambda i: (i, 0))
    const = lambda shape: pl.BlockSpec(shape, lambda i: (0,) * len(shape))
    return pl.pallas_call(
        _mixout_kernel,
        grid=(n // tm,),
        in_specs=[
            half, half, half,
            pl.BlockSpec((tm, HG_F), lambda i: (i, COL_HG // HG_F)),
            pl.BlockSpec((tm, D_MODEL), lambda i: (i, COL_GA // D_MODEL)),
            pl.BlockSpec((tm, D_MODEL), lambda i: (i, COL_GB // D_MODEL)),
            pl.BlockSpec((tm, D_MODEL), lambda i: (i, 0)),
            pl.BlockSpec((1, 1, D_MODEL), lambda i: (i // per_b, 0, 0)),
            const((1, LANES)), const((1, D_MODEL)),
            const((ATT_Q, D_MODEL)), const((HG_F, D_MODEL)), const((D_MODEL, D_MODEL)),
        ],
        out_specs=pl.BlockSpec((tm, D_MODEL), lambda i: (i, 0)),
        out_shape=jax.ShapeDtypeStruct((n, D_MODEL), F32),
        compiler_params=_params(("parallel",)),
        name="mixer_out",
    )(att, o_f, o_b, proj, proj, proj, x, gt, g_hgrn, g_norm, w_a, w_b, w_o)


def _router_kernel(x_ref, g_ref, sc_ref, sh_ref, wr_ref, br_ref, h_ref, sel_ref):
    h = _rms(x_ref[...], g_ref[...]) * (1.0 + sc_ref[0]) + sh_ref[0]
    h_ref[...] = h
    logits = jnp.dot(h, wr_ref[...], preferred_element_type=F32,
                     precision=lax.Precision.HIGHEST) + br_ref[...]
    lane = lax.broadcasted_iota(jnp.int32, logits.shape, 1).astype(F32)
    logits = jnp.where(lane < N_EXPERTS, logits, -jnp.inf)
    m1 = jnp.max(logits, axis=-1, keepdims=True)
    i1 = jnp.min(jnp.where(logits == m1, lane, float(LANES)), axis=-1, keepdims=True)
    rest = jnp.where(lane == i1, -jnp.inf, logits)
    m2 = jnp.max(rest, axis=-1, keepdims=True)
    i2 = jnp.min(jnp.where(rest == m2, lane, float(LANES)), axis=-1, keepdims=True)
    e2 = jnp.exp(m2 - m1)
    w1 = 1.0 / (1.0 + e2)
    w2 = e2 / (1.0 + e2)
    sel_ref[...] = (jnp.where(lane == SEL_E1, i1, 0.0) + jnp.where(lane == SEL_E2, i2, 0.0)
                    + jnp.where(lane == SEL_W1, w1, 0.0) + jnp.where(lane == SEL_W2, w2, 0.0))


def _router(x, g, sc, sh, w_r, b_r, seq, tm):
    n = x.shape[0]
    per_b = seq // tm
    return pl.pallas_call(
        _router_kernel,
        grid=(n // tm,),
        in_specs=[
            pl.BlockSpec((tm, D_MODEL), lambda i: (i, 0)),
            pl.BlockSpec((1, D_MODEL), lambda i: (0, 0)),
            pl.BlockSpec((1, 1, D_MODEL), lambda i: (i // per_b, 0, 0)),
            pl.BlockSpec((1, 1, D_MODEL), lambda i: (i // per_b, 0, 0)),
            pl.BlockSpec((D_MODEL, LANES), lambda i: (0, 0)),
            pl.BlockSpec((1, LANES), lambda i: (0, 0)),
        ],
        out_specs=[pl.BlockSpec((tm, D_MODEL), lambda i: (i, 0)),
                   pl.BlockSpec((tm, LANES), lambda i: (i, 0))],
        out_shape=[jax.ShapeDtypeStruct((n, D_MODEL), F32),
                   jax.ShapeDtypeStruct((n, LANES), F32)],
        compiler_params=_params(("parallel",)),
        name="router_top2",
    )(x, g, sc, sh, w_r, b_r)


def _route_plan(sel, tm, tmc):
    n = sel.shape[0]
    r = 2 * n
    flat_e = sel[:, :2].astype(jnp.int32).reshape(-1)
    onehot = (flat_e[:, None] == jnp.arange(N_EXPERTS, dtype=jnp.int32)[None, :]).astype(jnp.int32)
    csum = jnp.cumsum(onehot, axis=0)
    rank = jnp.sum(csum * onehot, axis=1) - 1
    counts = csum[-1]
    padded = (counts + tm - 1) // tm * tm
    ends = jnp.cumsum(padded)
    pos = (ends - padded)[flat_e] + rank
    n_tiles = r // tm + N_EXPERTS
    src = jnp.zeros((n_tiles * tm,), jnp.int32).at[pos].set(jnp.arange(r, dtype=jnp.int32) // 2)
    n_valid = (ends[-1] // tm).astype(jnp.int32)
    tile_start = jnp.arange(n_tiles, dtype=jnp.int32) * tm
    texp = jnp.minimum(jnp.searchsorted(ends, tile_start, side="right"), N_EXPERTS - 1)
    texp = jnp.where(tile_start < ends[-1], texp, texp[jnp.maximum(n_valid - 1, 0)])
    pos_tiles = pos.reshape(n // tmc, tmc, 2).transpose(0, 2, 1).reshape(n // tmc, 2 * tmc)
    return src.reshape(n_tiles, tm), texp.astype(jnp.int32), n_valid.reshape(1), pos_tiles


def _moe_kernel(texp_ref, nv_ref, src_hbm, h_hbm, wa_ref, wb_ref, wo_ref, y_ref,
                hbuf, hb16, idx_smem, gsem, isem, *, tm):
    t = pl.program_id(0)
    f = pl.program_id(1)
    nv = nv_ref[0]
    slot = t % 2
    other = 1 - slot

    def idx_copy(tile, s):
        return pltpu.make_async_copy(src_hbm.at[pl.ds(tile, 1)], idx_smem.at[pl.ds(s, 1)], isem.at[s])

    def issue_rows(s, lo, hi):
        def body(r, carry):
            tok = idx_smem[s, r]
            pltpu.make_async_copy(h_hbm.at[pl.ds(tok, 1)], hbuf.at[s, pl.ds(r, 1)], gsem.at[s]).start()
            return carry
        lax.fori_loop(lo, hi, body, 0, unroll=8)

    def wait_rows(s):
        pltpu.make_async_copy(h_hbm.at[pl.ds(0, tm)], hbuf.at[s], gsem.at[s]).wait()

    @pl.when((t == 0) & (f == 0) & (nv > 0))
    def _():
        first = idx_copy(0, 0)
        first.start()
        first.wait()
        issue_rows(0, 0, tm)

    @pl.when((f == 0) & (t < nv))
    def _():
        wait_rows(slot)
        hb16[...] = hbuf[slot].astype(BF16)

        @pl.when(t + 1 < nv)
        def _():
            idx_copy(t + 1, other).start()

    @pl.when((f == 1) & (t + 1 < nv))
    def _():
        idx_copy(t + 1, other).wait()
        issue_rows(other, 0, tm // 2)

    @pl.when((f == 2) & (t + 1 < nv))
    def _():
        issue_rows(other, tm // 2, tm)

    @pl.when(t < nv)
    def _():
        h = hb16[...]
        a = jnp.dot(h, wa_ref[0], preferred_element_type=F32)
        b = jnp.dot(h, wb_ref[0], preferred_element_type=F32)
        y = jnp.dot((_silu(a) * b).astype(BF16), wo_ref[0], preferred_element_type=F32)

        @pl.when(f == 0)
        def _():
            y_ref[...] = y

        @pl.when(f > 0)
        def _():
            y_ref[...] += y

    @pl.when((t >= nv) & (f == 0))
    def _():
        y_ref[...] = jnp.zeros(y_ref.shape, F32)


def _moe_experts(h, src, texp, n_valid, w_in, w_out, tm, chunk):
    n_tiles = src.shape[0]
    ff = w_in.shape[2] // 2
    n_fc = ff // chunk
    assert n_fc >= 3, "row gathers for the next tile are issued during chunks 1 and 2"

    def fidx(t, f, nv):
        return jnp.where(t < nv[0], f, n_fc - 1)

    grid_spec = pltpu.PrefetchScalarGridSpec(
        num_scalar_prefetch=2,
        grid=(n_tiles, n_fc),
        in_specs=[
            pl.BlockSpec(memory_space=pl.ANY),
            pl.BlockSpec(memory_space=pl.ANY),
            pl.BlockSpec((1, D_MODEL, chunk), lambda t, f, te, nv: (te[t], 0, fidx(t, f, nv))),
            pl.BlockSpec((1, D_MODEL, chunk), lambda t, f, te, nv: (te[t], 0, n_fc + fidx(t, f, nv))),
            pl.BlockSpec((1, chunk, D_MODEL), lambda t, f, te, nv: (te[t], fidx(t, f, nv), 0)),
        ],
        out_specs=pl.BlockSpec((tm, D_MODEL), lambda t, f, te, nv: (t, 0)),
        scratch_shapes=[
            pltpu.VMEM((2, tm, D_MODEL), F32),
            pltpu.VMEM((tm, D_MODEL), BF16),
            pltpu.SMEM((2, tm), jnp.int32),
            pltpu.SemaphoreType.DMA((2,)),
            pltpu.SemaphoreType.DMA((2,)),
        ],
    )
    return pl.pallas_call(
        functools.partial(_moe_kernel, tm=tm),
        grid_spec=grid_spec,
        out_shape=jax.ShapeDtypeStruct((n_tiles * tm, D_MODEL), F32),
        compiler_params=_params(("arbitrary", "arbitrary")),
        name="moe_experts",
    )(texp, n_valid, src, h, w_in, w_in, w_out)


def _combine_kernel(pos_hbm, y_hbm, x_ref, sel_ref, gt_ref, gn_ref, o_ref, ybuf, idx_smem, gsem, isem,
                    *, tmc):
    i = pl.program_id(0)
    fetch = pltpu.make_async_copy(pos_hbm.at[pl.ds(i, 1)], idx_smem, isem)
    fetch.start()
    fetch.wait()

    def body(r, carry):
        row = idx_smem[0, r]
        pltpu.make_async_copy(y_hbm.at[pl.ds(row, 1)], ybuf.at[pl.ds(r, 1)], gsem).start()
        return carry
    lax.fori_loop(0, 2 * tmc, body, 0, unroll=8)
    pltpu.make_async_copy(y_hbm.at[pl.ds(0, 2 * tmc)], ybuf, gsem).wait()

    sel = sel_ref[...]
    w1 = sel[:, SEL_W1:SEL_W1 + 1]
    w2 = sel[:, SEL_W2:SEL_W2 + 1]
    y = w1 * ybuf[0:tmc, :] + w2 * ybuf[tmc:2 * tmc, :]
    o_ref[...] = x_ref[...] + gt_ref[0] * _rms(y, gn_ref[...])


def _combine(pos_tiles, y_sorted, x, sel, gt, gn, seq, tmc):
    n = x.shape[0]
    per_b = seq // tmc
    return pl.pallas_call(
        functools.partial(_combine_kernel, tmc=tmc),
        grid=(n // tmc,),
        in_specs=[
            pl.BlockSpec(memory_space=pl.ANY),
            pl.BlockSpec(memory_space=pl.ANY),
            pl.BlockSpec((tmc, D_MODEL), lambda i: (i, 0)),
            pl.BlockSpec((tmc, LANES), lambda i: (i, 0)),
            pl.BlockSpec((1, 1, D_MODEL), lambda i: (i // per_b, 0, 0)),
            pl.BlockSpec((1, D_MODEL), lambda i: (0, 0)),
        ],
        out_specs=pl.BlockSpec((tmc, D_MODEL), lambda i: (i, 0)),
        out_shape=jax.ShapeDtypeStruct((n, D_MODEL), F32),
        scratch_shapes=[
            pltpu.VMEM((2 * tmc, D_MODEL), F32),
            pltpu.SMEM((1, 2 * tmc), jnp.int32),
            pltpu.SemaphoreType.DMA,
            pltpu.SemaphoreType.DMA,
        ],
        compiler_params=_params(("arbitrary",)),
        name="moe_combine",
    )(pos_tiles, y_sorted, x, sel, gt, gn)


def _ffn_kernel(x_ref, g_ref, sc_ref, sh_ref, gt_ref, gn_ref, wa_ref, wb_ref, wo_ref,
                o_ref, h_scr, acc_scr):
    f = pl.program_id(1)

    @pl.when(f == 0)
    def _():
        y = _rms(x_ref[...], g_ref[...])
        h_scr[...] = (y * (1.0 + sc_ref[0]) + sh_ref[0]).astype(BF16)
        acc_scr[...] = jnp.zeros(acc_scr.shape, F32)

    h = h_scr[...]
    a = jnp.dot(h, wa_ref[...], preferred_element_type=F32)
    b = jnp.dot(h, wb_ref[...], preferred_element_type=F32)
    acc_scr[...] += jnp.dot((_silu(a) * b).astype(BF16), wo_ref[...], preferred_element_type=F32)

    @pl.when(f == pl.num_programs(1) - 1)
    def _():
        o_ref[...] = x_ref[...] + gt_ref[0] * _rms(acc_scr[...], gn_ref[...])


def _ffn(x, g, sc, sh, gt, gn, w_in, w_out, seq, tm, chunk):
    n = x.shape[0]
    per_b = seq // tm
    n_fc = w_in.shape[1] // 2 // chunk
    vec = pl.BlockSpec((1, D_MODEL), lambda i, f: (0, 0))
    mod = pl.BlockSpec((1, 1, D_MODEL), lambda i, f: (i // per_b, 0, 0))
    return pl.pallas_call(
        _ffn_kernel,
        grid=(n // tm, n_fc),
        in_specs=[
            pl.BlockSpec((tm, D_MODEL), lambda i, f: (i, 0)), vec, mod, mod, mod, vec,
            pl.BlockSpec((D_MODEL, chunk), lambda i, f: (0, f)),
            pl.BlockSpec((D_MODEL, chunk), lambda i, f: (0, n_fc + f)),
            pl.BlockSpec((chunk, D_MODEL), lambda i, f: (f, 0)),
        ],
        out_specs=pl.BlockSpec((tm, D_MODEL), lambda i, f: (i, 0)),
        out_shape=jax.ShapeDtypeStruct((n, D_MODEL), F32),
        scratch_shapes=[pltpu.VMEM((tm, D_MODEL), BF16), pltpu.VMEM((tm, D_MODEL), F32)],
        compiler_params=_params(("parallel", "arbitrary")),
        name="dense_ffn",
    )(x, g, sc, sh, gt, gn, w_in, w_in, w_out)


def _rope_tables(seq):
    t = jnp.arange(seq, dtype=jnp.int32)
    row = (t // GRID_W).astype(F32)
    col = (t % GRID_W).astype(F32)
    inv = ROPE_THETA ** (-jnp.arange(ROPE_F, dtype=F32) / ROPE_F)
    ang_r = row[:, None] * inv
    ang_c = col[:, None] * inv
    cos = jnp.concatenate([jnp.cos(ang_r)] * 2 + [jnp.cos(ang_c)] * 2, axis=1)
    sin = jnp.concatenate([-jnp.sin(ang_r), jnp.sin(ang_r), -jnp.sin(ang_c), jnp.sin(ang_c)], axis=1)
    return jnp.tile(cos, (1, 2)), jnp.tile(sin, (1, 2))


def _prep_weights(w_in, g_qk, w_branch, w_o, w_ffn_in, w_ffn_out, w_router, b_router,
                  w_moe_in, w_moe_out):
    o_aq, o_ak, o_av, o_hq = 0, ATT_Q, ATT_Q + ATT_KV, ATT_Q + 2 * ATT_KV
    o_ga = o_hq + 5 * HG_F
    w_in_p = jnp.concatenate(
        [w_in[:, :, o_ga:], w_in[:, :, o_hq:o_ga], w_in[:, :, o_aq:o_hq]], axis=2).astype(BF16)
    pad_c = D_FF_PAD - D_FF
    w_fa = jnp.pad(w_ffn_in[:, :, :D_FF], ((0, 0), (0, 0), (0, pad_c)))
    w_fb = jnp.pad(w_ffn_in[:, :, D_FF:], ((0, 0), (0, 0), (0, pad_c)))
    w_ffn_in_p = jnp.concatenate([w_fa, w_fb], axis=2).astype(BF16)
    w_ffn_out_p = jnp.pad(w_ffn_out, ((0, 0), (0, pad_c), (0, 0))).astype(BF16)
    w_r_p = jnp.pad(w_router, ((0, 0), (0, 0), (0, LANES - N_EXPERTS)))
    b_r_p = jnp.pad(b_router, ((0, 0), (0, LANES - N_EXPERTS)))[:, None, :]
    g_q = jnp.tile(g_qk[:, 0, :], (1, LANES // HEAD_DIM))[:, None, :]
    g_k = jnp.tile(g_qk[:, 1, :], (1, LANES // HEAD_DIM))[:, None, :]
    return dict(w_in=w_in_p, w_branch=w_branch.astype(BF16), w_o=w_o.astype(BF16),
                w_ffn_in=w_ffn_in_p, w_ffn_out=w_ffn_out_p, w_r=w_r_p, b_r=b_r_p,
                w_moe_in=w_moe_in.astype(BF16), w_moe_out=w_moe_out.astype(BF16),
                g_q=g_q, g_k=g_k)


def _tile(seq, want):
    return min(seq, want)


def _trunk(x3, mods, g_norm, lb_logits, g_hgrn, wts):
    batch, seq, _ = x3.shape
    x = x3.reshape(batch * seq, D_MODEL)
    cos_t, sin_t = _rope_tables(seq)
    idx = np.arange(LANES)
    bd = jnp.asarray((idx[:, None] // HEAD_DIM == idx[None, :] // HEAD_DIM) / HEAD_DIM, BF16)
    tm_big = _tile(seq, 1024)
    tm = _tile(seq, 512)
    for l in range(DEPTH):
        mod = mods[l]
        sh1, sc1, gt1, sh2, sc2, gt2 = [m[:, None, :] for m in jnp.split(mod, 6, axis=-1)]
        gn = g_norm[l][:, None, :]
        proj = _inproj(x, gn[0], sc1, sh1, wts["w_in"][l], seq, tm_big)
        qh, ka, kb, vt = _rope(proj, cos_t, sin_t, wts["g_q"][l], wts["g_k"][l], bd, seq, tm)
        att = _attention(qh, ka, kb, vt, batch, seq, tm, tm)
        o_f = _hgrn(proj, lb_logits[0], batch, seq, tm, False, l)
        o_b = _hgrn(proj, lb_logits[1], batch, seq, tm, True, l)
        x = _mixout(att, o_f, o_b, proj, x, gt1, g_hgrn[l][None, :], gn[1],
                    wts["w_branch"][l, 0], wts["w_branch"][l, 1], wts["w_o"][l], seq, tm)
        if l % 2 == 0:
            x = _ffn(x, gn[2], sc2, sh2, gt2, gn[3], wts["w_ffn_in"][l // 2],
                     wts["w_ffn_out"][l // 2], seq, tm, FFN_CHUNK_DENSE)
        else:
            h, sel = _router(x, gn[2], sc2, sh2, wts["w_r"][l // 2], wts["b_r"][l // 2], seq, tm)
            moe_tile = min(MOE_TILE, 2 * batch * seq)
            src, texp, n_valid, pos_tiles = _route_plan(sel, moe_tile, tm)
            y_sorted = _moe_experts(h, src, texp, n_valid, wts["w_moe_in"][l // 2],
                                    wts["w_moe_out"][l // 2], moe_tile, FFN_CHUNK_EXPERT)
            x = _combine(pos_tiles, y_sorted, x, sel, gt2, gn[3], seq, tm)
    return x.reshape(batch, seq, D_MODEL)


def kernel(x_prompt, x_sample, c_prompt, c_sample, w_ada, b_ada, g_norm, w_in, g_qk, lb_logits,
           g_hgrn, w_branch, w_o, w_ffn_in, w_ffn_out, w_router, b_router, w_moe_in, w_moe_out):
    wts = _prep_weights(w_in, g_qk, w_branch, w_o, w_ffn_in, w_ffn_out, w_router, b_router,
                        w_moe_in, w_moe_out)
    n_p, n_s = c_prompt.shape[0], c_sample.shape[0]
    c_all = jnp.concatenate([c_prompt, c_sample], axis=0)
    pad = (-c_all.shape[0]) % SUBLANES
    c_all = jnp.pad(c_all, ((0, pad), (0, 0)))
    mods = _ada(c_all, w_ada, b_ada)
    y_p = _trunk(x_prompt, mods[:, :n_p], g_norm, lb_logits, g_hgrn, wts)
    y_s = _trunk(x_sample, mods[:, n_p:n_p + n_s], g_norm, lb_logits, g_hgrn, wts)
    return (y_p, y_s)
```

```python
import functools

import jax
import jax.numpy as jnp
import numpy as np
from jax import lax
from jax.experimental import pallas as pl
from jax.experimental.pallas import tpu as pltpu

F32 = jnp.float32
BF16 = jnp.bfloat16

D_MODEL = 1024
DEPTH = 4
GRID_W = 64
N_HEADS = 8
N_KV_HEADS = 2
HEAD_DIM = 64
KV_GROUP = N_HEADS // N_KV_HEADS
ROPE_F = HEAD_DIM // 4
ROPE_THETA = 10000.0
HG_HEADS = 4
HG_DK = 128
HG_DV = 128
ATT_Q = N_HEADS * HEAD_DIM
ATT_KV = N_KV_HEADS * HEAD_DIM
HG_F = HG_HEADS * HG_DK
D_FF = 2752
N_EXPERTS = 8
D_FF_EXPERT = 3584
EPS = 1e-6

LANES = 128
SUBLANES = 8
VMEM_LIMIT_BYTES = 56 * 1024 * 1024

COL_GA, COL_GB = 0, 1024
COL_HQ, COL_HFF, COL_HFB, COL_HI, COL_HG = 2048, 2560, 3072, 3584, 4096
COL_AQ, COL_AK, COL_AV = 4608, 5120, 5248
IN_COLS = 5376
IN_TILE_N = 1792

D_FF_PAD = 2816
FFN_CHUNK_DENSE = 1408
FFN_CHUNK_EXPERT = 1792
HG_CHUNK = 128
VT_ROWS = HEAD_DIM + 16
MOE_TILE = 512
SEL_E1, SEL_E2, SEL_W1, SEL_W2 = 0, 1, 2, 3


def _params(sem):
    return pltpu.CompilerParams(dimension_semantics=sem, vmem_limit_bytes=VMEM_LIMIT_BYTES)


def _sigmoid(x):
    return 1.0 / (1.0 + jnp.exp(-x))


def _silu(x):
    return x * _sigmoid(x)


def _rms(x, g):
    ms = jnp.mean(x * x, axis=-1, keepdims=True)
    return x * lax.rsqrt(ms + EPS) * g


def _ada_kernel(c_ref, w_ref, b_ref, o_ref):
    c = c_ref[...]
    o_ref[0] = jnp.dot(_silu(c), w_ref[0], preferred_element_type=F32,
                       precision=lax.Precision.HIGHEST) + b_ref[0]


def _ada(c, w_ada, b_ada):
    bp = c.shape[0]
    n_col = 6 * D_MODEL // D_MODEL
    return pl.pallas_call(
        _ada_kernel,
        grid=(DEPTH, n_col),
        in_specs=[
            pl.BlockSpec((bp, D_MODEL), lambda l, j: (0, 0)),
            pl.BlockSpec((1, D_MODEL, D_MODEL), lambda l, j: (l, 0, j)),
            pl.BlockSpec((1, 1, D_MODEL), lambda l, j: (l, 0, j)),
        ],
        out_specs=pl.BlockSpec((1, bp, D_MODEL), lambda l, j: (l, 0, j)),
        out_shape=jax.ShapeDtypeStruct((DEPTH, bp, 6 * D_MODEL), F32),
        compiler_params=_params(("arbitrary", "arbitrary")),
        name="ada_mod",
    )(c, w_ada, b_ada.reshape(DEPTH, 1, 6 * D_MODEL))


def _inproj_kernel(x_ref, g_ref, sc_ref, sh_ref, w_ref, o_ref, h_scr):
    @pl.when(pl.program_id(1) == 0)
    def _():
        y = _rms(x_ref[...], g_ref[...])
        h_scr[...] = (y * (1.0 + sc_ref[0]) + sh_ref[0]).astype(BF16)

    o_ref[...] = jnp.dot(h_scr[...], w_ref[...], preferred_element_type=F32).astype(BF16)


def _inproj(x, g, sc, sh, w, seq, tm):
    n = x.shape[0]
    per_b = seq // tm
    return pl.pallas_call(
        _inproj_kernel,
        grid=(n // tm, IN_COLS // IN_TILE_N),
        in_specs=[
            pl.BlockSpec((tm, D_MODEL), lambda i, j: (i, 0)),
            pl.BlockSpec((1, D_MODEL), lambda i, j: (0, 0)),
            pl.BlockSpec((1, 1, D_MODEL), lambda i, j: (i // per_b, 0, 0)),
            pl.BlockSpec((1, 1, D_MODEL), lambda i, j: (i // per_b, 0, 0)),
            pl.BlockSpec((D_MODEL, IN_TILE_N), lambda i, j: (0, j)),
        ],
        out_specs=pl.BlockSpec((tm, IN_TILE_N), lambda i, j: (i, j)),
        out_shape=jax.ShapeDtypeStruct((n, IN_COLS), BF16),
        scratch_shapes=[pltpu.VMEM((tm, D_MODEL), BF16)],
        compiler_params=_params(("parallel", "arbitrary")),
        name="in_proj",
    )(x, g, sc, sh, w)


def _rope_kernel(aq_ref, ak_ref, av_ref, cos_ref, sin_ref, gq_ref, gk_ref, bd_ref,
                 qh_ref, ka_ref, kb_ref, vt_ref):
    tm = aq_ref.shape[0]
    lane = lax.broadcasted_iota(jnp.int32, (tm, LANES), 1)
    first_half = (lane % (2 * ROPE_F)) < ROPE_F
    low = lane < HEAD_DIM
    cos = cos_ref[...]
    sin = sin_ref[...]
    bd = bd_ref[...]

    def norm_rope(x, g):
        xx = x * x
        hi = xx.astype(BF16)
        lo = (xx - hi.astype(F32)).astype(BF16)
        ms = (jnp.dot(hi, bd, preferred_element_type=F32)
              + jnp.dot(lo, bd, preferred_element_type=F32))
        y = x * lax.rsqrt(ms + EPS) * g
        nxt = pltpu.roll(y, LANES - ROPE_F, 1)
        prv = pltpu.roll(y, ROPE_F, 1)
        return y * cos + jnp.where(first_half, nxt, prv) * sin

    gq = gq_ref[...]
    scale = HEAD_DIM ** -0.5 * float(np.log2(np.e))
    for j in range(ATT_Q // LANES):
        r = norm_rope(aq_ref[:, j * LANES:(j + 1) * LANES].astype(F32), gq) * scale
        qh_ref[:, (2 * j) * LANES:(2 * j + 1) * LANES] = jnp.where(low, r, 0.0).astype(BF16)
        qh_ref[:, (2 * j + 1) * LANES:(2 * j + 2) * LANES] = jnp.where(low, 0.0, r).astype(BF16)

    k = norm_rope(ak_ref[...].astype(F32), gk_ref[...])
    ka_ref[...] = k.astype(BF16)
    kb_ref[...] = pltpu.roll(k, HEAD_DIM, 1).astype(BF16)
    vt = av_ref[...].astype(F32).T.astype(BF16)
    ones = jnp.ones((VT_ROWS - HEAD_DIM, tm), BF16)
    for g in range(N_KV_HEADS):
        vt_ref[g, 0:HEAD_DIM, :] = vt[g * HEAD_DIM:(g + 1) * HEAD_DIM, :]
        vt_ref[g, HEAD_DIM:VT_ROWS, :] = ones


def _rope(proj, cos_t, sin_t, gq, gk, bd, seq, tm):
    n = proj.shape[0]
    per_b = seq // tm
    kv_spec = pl.BlockSpec((tm, LANES), lambda i: (i, 0))
    vt_spec = pl.BlockSpec((N_KV_HEADS, VT_ROWS, tm), lambda i: (0, 0, i))
    return pl.pallas_call(
        _rope_kernel,
        grid=(n // tm,),
        in_specs=[
            pl.BlockSpec((tm, ATT_Q), lambda i: (i, COL_AQ // ATT_Q)),
            pl.BlockSpec((tm, LANES), lambda i: (i, COL_AK // LANES)),
            pl.BlockSpec((tm, LANES), lambda i: (i, COL_AV // LANES)),
            pl.BlockSpec((tm, LANES), lambda i: (i % per_b, 0)),
            pl.BlockSpec((tm, LANES), lambda i: (i % per_b, 0)),
            pl.BlockSpec((1, LANES), lambda i: (0, 0)),
            pl.BlockSpec((1, LANES), lambda i: (0, 0)),
            pl.BlockSpec((LANES, LANES), lambda i: (0, 0)),
        ],
        out_specs=[pl.BlockSpec((tm, N_HEADS * LANES), lambda i: (i, 0)),
                   kv_spec, kv_spec, vt_spec],
        out_shape=[jax.ShapeDtypeStruct((n, N_HEADS * LANES), BF16)]
        + [jax.ShapeDtypeStruct((n, LANES), BF16)] * 2
        + [jax.ShapeDtypeStruct((N_KV_HEADS, VT_ROWS, n), BF16)],
        compiler_params=_params(("parallel",)),
        name="qk_norm_rope",
    )(proj, proj, proj, cos_t, sin_t, gq, gk, bd)


def _attn_kernel(q_ref, ka_ref, kb_ref, vt_ref, o_ref, m_scr, acc_scr):
    j = pl.program_id(2)

    @pl.when(j == 0)
    def _():
        m_scr[...] = jnp.full(m_scr.shape, -jnp.inf, F32)
        acc_scr[...] = jnp.zeros(acc_scr.shape, F32)

    tq = q_ref.shape[0]
    pairs = [(h, h + 2) for h in range(N_HEADS) if h % 4 < 2]

    def scores(pair):
        g, half = pair[0] // KV_GROUP, pair[0] % 2
        k = (ka_ref if g == half else kb_ref)[...]
        q = jnp.concatenate([q_ref[:, h * LANES:(h + 1) * LANES] for h in pair], axis=0)
        return lax.dot_general(k, q, (((1,), (1,)), ((), ())), preferred_element_type=F32)

    st_next = scores(pairs[0])
    for i, pair in enumerate(pairs):
        vt = vt_ref[pair[0] // KV_GROUP]
        st = st_next
        if i + 1 < len(pairs):
            st_next = scores(pairs[i + 1])
        m_prev = m_scr[i]
        m_new = jnp.maximum(m_prev, jnp.max(st, axis=0, keepdims=True))
        alpha = jnp.exp2(m_prev - m_new)
        pt = jnp.exp2((st - m_new).astype(BF16))
        acc_scr[i] = alpha * acc_scr[i] + jnp.dot(vt, pt, preferred_element_type=F32)
        m_scr[i] = m_new

    @pl.when(j == pl.num_programs(2) - 1)
    def _():
        def head_out(h):
            i = pairs.index(next(p for p in pairs if h in p))
            acc = acc_scr[i][:, pairs[i].index(h) * tq:(pairs[i].index(h) + 1) * tq]
            return acc[0:HEAD_DIM, :] / acc[HEAD_DIM:HEAD_DIM + 1, :]

        for jj in range(N_HEADS // 2):
            both = jnp.concatenate([head_out(2 * jj), head_out(2 * jj + 1)], axis=0)
            o_ref[:, jj * LANES:(jj + 1) * LANES] = both.T.astype(BF16)


def _attention(qh, ka, kb, vt, batch, seq, tq, tk):
    n = qh.shape[0]
    nq, nk = seq // tq, seq // tk
    k_spec = pl.BlockSpec((tk, LANES), lambda b, i, j: (b * nk + j, 0))
    vt_spec = pl.BlockSpec((N_KV_HEADS, VT_ROWS, tk), lambda b, i, j: (0, 0, b * nk + j))
    return pl.pallas_call(
        _attn_kernel,
        grid=(batch, nq, nk),
        in_specs=[pl.BlockSpec((tq, N_HEADS * LANES), lambda b, i, j: (b * nq + i, 0)),
                  k_spec, k_spec, vt_spec],
        out_specs=pl.BlockSpec((tq, ATT_Q), lambda b, i, j: (b * nq + i, 0)),
        out_shape=jax.ShapeDtypeStruct((n, ATT_Q), BF16),
        scratch_shapes=[pltpu.VMEM((N_HEADS // 2, 1, 2 * tq), F32),
                        pltpu.VMEM((N_HEADS // 2, VT_ROWS, 2 * tq), F32)],
        compiler_params=_params(("parallel", "parallel", "arbitrary")),
        name="gqa_attention",
    )(qh, ka, kb, vt)


def _level_ref(b, m, rev):
    c = b.shape[0]
    g = 2 * m
    if g >= SUBLANES:
        b3 = b.reshape(c // g, g, LANES)
        r = m if rev else m - 1
        return jnp.broadcast_to(b3[:, r:r + 1, :], b3.shape).reshape(c, LANES)
    b3 = b.reshape(c // SUBLANES, SUBLANES, LANES)
    sub = lax.broadcasted_iota(jnp.int32, b3.shape, 1)
    out = None
    for p0 in range(0, SUBLANES, g):
        r = p0 + (m if rev else m - 1)
        cand = jnp.broadcast_to(b3[:, r:r + 1, :], b3.shape)
        out = cand if out is None else jnp.where(sub >= p0, cand, out)
    return out.reshape(c, LANES)


def _chunk_cumsum(x, tri):
    hi = x.astype(BF16)
    r1 = x - hi.astype(F32)
    mid = r1.astype(BF16)
    lo = (r1 - mid.astype(F32)).astype(BF16)
    return (jnp.dot(tri, hi, preferred_element_type=F32)
            + jnp.dot(tri, mid, preferred_element_type=F32)
            + jnp.dot(tri, lo, preferred_element_type=F32))


def _neg_abs(z):
    bits = lax.bitcast_convert_type(z, jnp.int32) | jnp.int32(-2 ** 31)
    return lax.bitcast_convert_type(bits, F32)


def _hgrn_kernel(hq_ref, hf_ref, hi_ref, lbl_ref, o_ref, st_scr, *, rev, layer, chunk):
    c = chunk
    n_chunks = hq_ref.shape[0] // c

    @pl.when(pl.program_id(1) == 0)
    def _():
        st_scr[...] = jnp.zeros(st_scr.shape, F32)

    lbl = lbl_ref[...]
    lbe = jnp.exp(lbl - jnp.max(lbl, axis=0, keepdims=True))
    lbp = lbe / jnp.sum(lbe, axis=0, keepdims=True)
    lb_all = jnp.zeros((1, HG_F), F32)
    for l in range(1, layer + 1):
        lb_all = lb_all + lbp[l:l + 1, :]

    row = lax.broadcasted_iota(jnp.int32, (c, LANES), 0)
    row_a = lax.broadcasted_iota(jnp.int32, (c, c), 0)
    col_a = lax.broadcasted_iota(jnp.int32, (c, c), 1)
    tri = jnp.where((row_a <= col_a) if rev else (row_a >= col_a), 1.0, 0.0).astype(BF16)
    lev = 31 - lax.clz(row_a ^ col_a)
    lev = jnp.where((row_a < col_a) if rev else (row_a > col_a), lev, -1)
    q_scale = HG_DK ** -0.5
    log2e = float(np.log2(np.e))
    nt_dims = (((1,), (1,)), ((), ()))

    def body(ci, carry):
        cc = (n_chunks - 1 - ci) if rev else ci
        r0 = pl.multiple_of(cc * c, c)
        heads = range(HG_HEADS)
        cols = [slice(h * LANES, (h + 1) * LANES) for h in heads]
        v = [hi_ref[pl.ds(r0, c), cols[h]] for h in heads]
        q = [_silu(hq_ref[pl.ds(r0, c), cols[h]].astype(F32)) * q_scale for h in heads]
        f = [lb_all[:, cols[h]] + (1.0 - lb_all[:, cols[h]])
             * _sigmoid(hf_ref[pl.ds(r0, c), cols[h]].astype(F32)) for h in heads]
        k = [1.0 - f[h] for h in heads]
        b = [_chunk_cumsum(jnp.log(f[h]) * log2e, tri) for h in heads]
        a = [jnp.zeros((c, c), F32) for h in heads]
        m = c // 2
        while m >= 1:
            upper = (row & m) != 0
            for h in heads:
                e = jnp.exp2(_neg_abs(b[h] - _level_ref(b[h], m, rev)).astype(BF16))
                w = jnp.where(upper, k[h], q[h]) if rev else jnp.where(upper, q[h], k[h])
                w = w.astype(BF16) * e
                al = lax.dot_general(w, w, nt_dims, preferred_element_type=F32)
                a[h] = jnp.where(lev == (m.bit_length() - 1), al, a[h])
            m //= 2
        for h in heads:
            st = st_scr[h]
            o = lax.dot_general((q[h] * jnp.exp2(b[h])).astype(BF16), st.astype(BF16), nt_dims,
                                preferred_element_type=F32)
            diag = jnp.sum(q[h] * k[h], axis=-1, keepdims=True)
            o = (o + jnp.dot(a[h].astype(BF16), v[h], preferred_element_type=F32)
                 + diag * v[h].astype(F32))
            o_ref[pl.ds(r0, c), cols[h]] = o.astype(o_ref.dtype)
            b_end = b[h][0:1, :] if rev else b[h][c - 1:c, :]
            k_end = (k[h] * jnp.exp2(b_end - b[h])).astype(BF16)
            upd = lax.dot_general(v[h], k_end, (((0,), (0,)), ((), ())), preferred_element_type=F32)
            st_scr[h] = st * jnp.exp2(b_end) + upd
        return carry

    lax.fori_loop(0, n_chunks, body, 0, unroll=True)


def _hgrn(proj, lb_logits_dir, batch, seq, tt, rev, layer):
    n = proj.shape[0]
    nt = seq // tt
    col_f = (COL_HFB if rev else COL_HFF) // HG_F

    def rows(b, i):
        return b * nt + ((nt - 1 - i) if rev else i)

    return pl.pallas_call(
        functools.partial(_hgrn_kernel, rev=rev, layer=layer, chunk=min(HG_CHUNK, tt)),
        grid=(batch, nt),
        in_specs=[
            pl.BlockSpec((tt, HG_F), lambda b, i: (rows(b, i), COL_HQ // HG_F)),
            pl.BlockSpec((tt, HG_F), lambda b, i: (rows(b, i), col_f)),
            pl.BlockSpec((tt, HG_F), lambda b, i: (rows(b, i), COL_HI // HG_F)),
            pl.BlockSpec((DEPTH, HG_F), lambda b, i: (0, 0)),
        ],
        out_specs=pl.BlockSpec((tt, HG_F), lambda b, i: (rows(b, i), 0)),
        out_shape=jax.ShapeDtypeStruct((n, HG_F), BF16),
        scratch_shapes=[pltpu.VMEM((HG_HEADS, HG_DV, HG_DK), F32)],
        compiler_params=_params(("parallel", "arbitrary")),
        name="hgrn2_bwd" if rev else "hgrn2_fwd",
    )(proj, proj, proj, lb_logits_dir)


def _mixout_kernel(att_ref, of_ref, ob_ref, hg_ref, ga_ref, gb_ref, x_ref, gt_ref, ghg_ref,
                   gn_ref, wa_ref, wb_ref, wo_ref, o_ref):
    ghg = ghg_ref[...]
    recs = []
    for h in range(HG_HEADS):
        cols = slice(h * LANES, (h + 1) * LANES)
        o = of_ref[:, cols].astype(F32) + ob_ref[:, cols].astype(F32)
        recs.append((_rms(o, ghg) * _silu(hg_ref[:, cols].astype(F32))).astype(BF16))
    rec = jnp.concatenate(recs, axis=1)
    pa = jnp.dot(att_ref[...], wa_ref[...], preferred_element_type=F32)
    pb = jnp.dot(rec, wb_ref[...], preferred_element_type=F32)
    merged = (_sigmoid(ga_ref[...].astype(F32)) * pa + _sigmoid(gb_ref[...].astype(F32)) * pb)
    y = jnp.dot(merged.astype(BF16), wo_ref[...], preferred_element_type=F32)
    o_ref[...] = x_ref[...] + gt_ref[0] * _rms(y, gn_ref[...])


def _mixout(att, o_f, o_b, proj, x, gt, g_hgrn, g_norm, w_a, w_b, w_o, seq, tm):
    n = x.shape[0]
    per_b = seq // tm
    half = pl.BlockSpec((tm, ATT_Q), lambda i: (i, 0))
    const = lambda shape: pl.BlockSpec(shape, lambda i: (0,) * len(shape))
    return pl.pallas_call(
        _mixout_kernel,
        grid=(n // tm,),
        in_specs=[
            half, half, half,
            pl.BlockSpec((tm, HG_F), lambda i: (i, COL_HG // HG_F)),
            pl.BlockSpec((tm, D_MODEL), lambda i: (i, COL_GA // D_MODEL)),
            pl.BlockSpec((tm, D_MODEL), lambda i: (i, COL_GB // D_MODEL)),
            pl.BlockSpec((tm, D_MODEL), lambda i: (i, 0)),
            pl.BlockSpec((1, 1, D_MODEL), lambda i: (i // per_b, 0, 0)),
            const((1, LANES)), const((1, D_MODEL)),
            const((ATT_Q, D_MODEL)), const((HG_F, D_MODEL)), const((D_MODEL, D_MODEL)),
        ],
        out_specs=pl.BlockSpec((tm, D_MODEL), lambda i: (i, 0)),
        out_shape=jax.ShapeDtypeStruct((n, D_MODEL), F32),
        compiler_params=_params(("parallel",)),
        name="mixer_out",
    )(att, o_f, o_b, proj, proj, proj, x, gt, g_hgrn, g_norm, w_a, w_b, w_o)


def _router_kernel(x_ref, g_ref, sc_ref, sh_ref, wr_ref, br_ref, h_ref, sel_ref):
    h = _rms(x_ref[...], g_ref[...]) * (1.0 + sc_ref[0]) + sh_ref[0]
    tm = h.shape[0]
    for j in range(D_MODEL // LANES):
        h_ref[pl.ds(j, tm, stride=SUBLANES), :] = h[:, j * LANES:(j + 1) * LANES]
    logits = jnp.dot(h, wr_ref[...], preferred_element_type=F32,
                     precision=lax.Precision.HIGHEST) + br_ref[...]
    lane = lax.broadcasted_iota(jnp.int32, logits.shape, 1).astype(F32)
    logits = jnp.where(lane < N_EXPERTS, logits, -jnp.inf)
    m1 = jnp.max(logits, axis=-1, keepdims=True)
    i1 = jnp.min(jnp.where(logits == m1, lane, float(LANES)), axis=-1, keepdims=True)
    rest = jnp.where(lane == i1, -jnp.inf, logits)
    m2 = jnp.max(rest, axis=-1, keepdims=True)
    i2 = jnp.min(jnp.where(rest == m2, lane, float(LANES)), axis=-1, keepdims=True)
    e2 = jnp.exp(m2 - m1)
    w1 = 1.0 / (1.0 + e2)
    w2 = e2 / (1.0 + e2)
    sel_ref[...] = (jnp.where(lane == SEL_E1, i1, 0.0) + jnp.where(lane == SEL_E2, i2, 0.0)
                    + jnp.where(lane == SEL_W1, w1, 0.0) + jnp.where(lane == SEL_W2, w2, 0.0))


def _router(x, g, sc, sh, w_r, b_r, seq, tm):
    n = x.shape[0]
    per_b = seq // tm
    return pl.pallas_call(
        _router_kernel,
        grid=(n // tm,),
        in_specs=[
            pl.BlockSpec((tm, D_MODEL), lambda i: (i, 0)),
            pl.BlockSpec((1, D_MODEL), lambda i: (0, 0)),
            pl.BlockSpec((1, 1, D_MODEL), lambda i: (i // per_b, 0, 0)),
            pl.BlockSpec((1, 1, D_MODEL), lambda i: (i // per_b, 0, 0)),
            pl.BlockSpec((D_MODEL, LANES), lambda i: (0, 0)),
            pl.BlockSpec((1, LANES), lambda i: (0, 0)),
        ],
        out_specs=[pl.BlockSpec((tm * SUBLANES, LANES), lambda i: (i, 0)),
                   pl.BlockSpec((tm, LANES), lambda i: (i, 0))],
        out_shape=[jax.ShapeDtypeStruct((n * SUBLANES, LANES), F32),
                   jax.ShapeDtypeStruct((n, LANES), F32)],
        compiler_params=_params(("parallel",)),
        name="router_top2",
    )(x, g, sc, sh, w_r, b_r)


def _route_plan(sel, tm, tmc):
    n = sel.shape[0]
    r = 2 * n
    flat_e = sel[:, :2].astype(jnp.int32).reshape(-1)
    onehot = (flat_e[:, None] == jnp.arange(N_EXPERTS, dtype=jnp.int32)[None, :]).astype(jnp.int32)
    csum = jnp.cumsum(onehot, axis=0)
    rank = jnp.sum(csum * onehot, axis=1) - 1
    counts = csum[-1]
    padded = (counts + tm - 1) // tm * tm
    ends = jnp.cumsum(padded)
    pos = (ends - padded)[flat_e] + rank
    n_tiles = r // tm + N_EXPERTS
    src = jnp.zeros((n_tiles * tm,), jnp.int32).at[pos].set(jnp.arange(r, dtype=jnp.int32) // 2)
    src = src * SUBLANES
    pos = pos * SUBLANES
    n_valid = (ends[-1] // tm).astype(jnp.int32)
    tile_start = jnp.arange(n_tiles, dtype=jnp.int32) * tm
    texp = jnp.minimum(jnp.searchsorted(ends, tile_start, side="right"), N_EXPERTS - 1)
    texp = jnp.where(tile_start < ends[-1], texp, texp[jnp.maximum(n_valid - 1, 0)])
    pos_tiles = pos.reshape(n // tmc, tmc, 2).transpose(0, 2, 1).reshape(n // tmc, 2 * tmc)
    return src.reshape(n_tiles, tm), texp.astype(jnp.int32), n_valid.reshape(1), pos_tiles


def _moe_kernel(texp_ref, nv_ref, src_hbm, h_hbm, wa_ref, wb_ref, wo_ref, y_ref,
                hbuf, hb16, acc, idx_smem, gsem, isem, *, tm):
    t = pl.program_id(0)
    f = pl.program_id(1)
    nv = nv_ref[0]
    slot = t % 2
    other = 1 - slot
    n_lane_blocks = D_MODEL // LANES

    def idx_copy(tile, s):
        return pltpu.make_async_copy(src_hbm.at[pl.ds(tile, 1)], idx_smem.at[pl.ds(s, 1)], isem.at[s])

    def issue_rows(s):
        def body(r, carry):
            src_row = pl.multiple_of(idx_smem[s, r], SUBLANES)
            dst_row = pl.multiple_of(r * SUBLANES, SUBLANES)
            pltpu.make_async_copy(h_hbm.at[pl.ds(src_row, SUBLANES)],
                                  hbuf.at[s, pl.ds(dst_row, SUBLANES)], gsem.at[s]).start()
            return carry
        lax.fori_loop(0, tm, body, 0, unroll=8)

    def wait_rows(s):
        pltpu.make_async_copy(h_hbm.at[pl.ds(0, tm * SUBLANES)], hbuf.at[s], gsem.at[s]).wait()

    @pl.when((t == 0) & (f == 0) & (nv > 0))
    def _():
        first = idx_copy(0, 0)
        first.start()
        first.wait()
        issue_rows(0)

    @pl.when((f == 0) & (t < nv))
    def _():
        wait_rows(slot)
        for j in range(n_lane_blocks):
            hb16[:, j * LANES:(j + 1) * LANES] = (
                hbuf[slot, pl.ds(j, tm, stride=SUBLANES), :].astype(BF16))

        @pl.when(t + 1 < nv)
        def _():
            idx_copy(t + 1, other).start()

    @pl.when((f == 1) & (t + 1 < nv))
    def _():
        idx_copy(t + 1, other).wait()
        issue_rows(other)

    @pl.when(t < nv)
    def _():
        h = hb16[...]
        a = jnp.dot(h, wa_ref[0], preferred_element_type=F32)
        b = jnp.dot(h, wb_ref[0], preferred_element_type=F32)
        y = jnp.dot((_silu(a) * b).astype(BF16), wo_ref[0], preferred_element_type=F32)

        @pl.when(f == 0)
        def _():
            acc[...] = y

        @pl.when(f == 1)
        def _():
            for j in range(n_lane_blocks):
                cols = slice(j * LANES, (j + 1) * LANES)
                y_ref[pl.ds(j, tm, stride=SUBLANES), :] = acc[:, cols] + y[:, cols]

    @pl.when((t >= nv) & (f == 0))
    def _():
        y_ref[...] = jnp.zeros(y_ref.shape, F32)


def _moe_experts(h, src, texp, n_valid, w_in, w_out, tm, chunk):
    n_tiles = src.shape[0]
    ff = w_in.shape[2] // 2
    n_fc = ff // chunk
    assert n_fc == 2, "the kernel's gather schedule and accumulation are written for two chunks"

    def fidx(t, f, nv):
        return jnp.where(t < nv[0], f, n_fc - 1)

    grid_spec = pltpu.PrefetchScalarGridSpec(
        num_scalar_prefetch=2,
        grid=(n_tiles, n_fc),
        in_specs=[
            pl.BlockSpec(memory_space=pl.ANY),
            pl.BlockSpec(memory_space=pl.ANY),
            pl.BlockSpec((1, D_MODEL, chunk), lambda t, f, te, nv: (te[t], 0, fidx(t, f, nv))),
            pl.BlockSpec((1, D_MODEL, chunk), lambda t, f, te, nv: (te[t], 0, n_fc + fidx(t, f, nv))),
            pl.BlockSpec((1, chunk, D_MODEL), lambda t, f, te, nv: (te[t], fidx(t, f, nv), 0)),
        ],
        out_specs=pl.BlockSpec((tm * SUBLANES, LANES), lambda t, f, te, nv: (t, 0)),
        scratch_shapes=[
            pltpu.VMEM((2, tm * SUBLANES, LANES), F32),
            pltpu.VMEM((tm, D_MODEL), BF16),
            pltpu.VMEM((tm, D_MODEL), F32),
            pltpu.SMEM((2, tm), jnp.int32),
            pltpu.SemaphoreType.DMA((2,)),
            pltpu.SemaphoreType.DMA((2,)),
        ],
    )
    return pl.pallas_call(
        functools.partial(_moe_kernel, tm=tm),
        grid_spec=grid_spec,
        out_shape=jax.ShapeDtypeStruct((n_tiles * tm * SUBLANES, LANES), F32),
        compiler_params=_params(("arbitrary", "arbitrary")),
        name="moe_experts",
    )(texp, n_valid, src, h, w_in, w_in, w_out)


def _combine_kernel(pos_hbm, y_hbm, x_ref, sel_ref, gt_ref, gn_ref, o_ref, ybuf, idx_smem, gsem, isem,
                    *, tmc):
    i = pl.program_id(0)
    fetch = pltpu.make_async_copy(pos_hbm.at[pl.ds(i, 1)], idx_smem, isem)
    fetch.start()
    fetch.wait()

    def body(r, carry):
        src_row = pl.multiple_of(idx_smem[0, r], SUBLANES)
        dst_row = pl.multiple_of(r * SUBLANES, SUBLANES)
        pltpu.make_async_copy(y_hbm.at[pl.ds(src_row, SUBLANES)],
                              ybuf.at[pl.ds(dst_row, SUBLANES)], gsem).start()
        return carry
    lax.fori_loop(0, 2 * tmc, body, 0, unroll=8)
    pltpu.make_async_copy(y_hbm.at[pl.ds(0, 2 * tmc * SUBLANES)], ybuf, gsem).wait()

    sel = sel_ref[...]
    w1 = sel[:, SEL_W1:SEL_W1 + 1]
    w2 = sel[:, SEL_W2:SEL_W2 + 1]
    second = tmc * SUBLANES
    y = jnp.concatenate(
        [w1 * ybuf[pl.ds(j, tmc, stride=SUBLANES), :]
         + w2 * ybuf[pl.ds(second + j, tmc, stride=SUBLANES), :]
         for j in range(D_MODEL // LANES)], axis=1)
    o_ref[...] = x_ref[...] + gt_ref[0] * _rms(y, gn_ref[...])


def _combine(pos_tiles, y_sorted, x, sel, gt, gn, seq, tmc):
    n = x.shape[0]
    per_b = seq // tmc
    return pl.pallas_call(
        functools.partial(_combine_kernel, tmc=tmc),
        grid=(n // tmc,),
        in_specs=[
            pl.BlockSpec(memory_space=pl.ANY),
            pl.BlockSpec(memory_space=pl.ANY),
            pl.BlockSpec((tmc, D_MODEL), lambda i: (i, 0)),
            pl.BlockSpec((tmc, LANES), lambda i: (i, 0)),
            pl.BlockSpec((1, 1, D_MODEL), lambda i: (i // per_b, 0, 0)),
            pl.BlockSpec((1, D_MODEL), lambda i: (0, 0)),
        ],
        out_specs=pl.BlockSpec((tmc, D_MODEL), lambda i: (i, 0)),
        out_shape=jax.ShapeDtypeStruct((n, D_MODEL), F32),
        scratch_shapes=[
            pltpu.VMEM((2 * tmc * SUBLANES, LANES), F32),
            pltpu.SMEM((1, 2 * tmc), jnp.int32),
            pltpu.SemaphoreType.DMA,
            pltpu.SemaphoreType.DMA,
        ],
        compiler_params=_params(("arbitrary",)),
        name="moe_combine",
    )(pos_tiles, y_sorted, x, sel, gt, gn)


def _ffn_kernel(x_ref, g_ref, sc_ref, sh_ref, gt_ref, gn_ref, wa_ref, wb_ref, wo_ref,
                o_ref, h_scr, acc_scr):
    f = pl.program_id(1)

    @pl.when(f == 0)
    def _():
        y = _rms(x_ref[...], g_ref[...])
        h_scr[...] = (y * (1.0 + sc_ref[0]) + sh_ref[0]).astype(BF16)
        acc_scr[...] = jnp.zeros(acc_scr.shape, F32)

    h = h_scr[...]
    a = jnp.dot(h, wa_ref[...], preferred_element_type=F32)
    b = jnp.dot(h, wb_ref[...], preferred_element_type=F32)
    acc_scr[...] += jnp.dot((_silu(a) * b).astype(BF16), wo_ref[...], preferred_element_type=F32)

    @pl.when(f == pl.num_programs(1) - 1)
    def _():
        o_ref[...] = x_ref[...] + gt_ref[0] * _rms(acc_scr[...], gn_ref[...])


def _ffn(x, g, sc, sh, gt, gn, w_in, w_out, seq, tm, chunk):
    n = x.shape[0]
    per_b = seq // tm
    n_fc = w_in.shape[1] // 2 // chunk
    vec = pl.BlockSpec((1, D_MODEL), lambda i, f: (0, 0))
    mod = pl.BlockSpec((1, 1, D_MODEL), lambda i, f: (i // per_b, 0, 0))
    return pl.pallas_call(
        _ffn_kernel,
        grid=(n // tm, n_fc),
        in_specs=[
            pl.BlockSpec((tm, D_MODEL), lambda i, f: (i, 0)), vec, mod, mod, mod, vec,
            pl.BlockSpec((D_MODEL, chunk), lambda i, f: (0, f)),
            pl.BlockSpec((D_MODEL, chunk), lambda i, f: (0, n_fc + f)),
            pl.BlockSpec((chunk, D_MODEL), lambda i, f: (f, 0)),
        ],
        out_specs=pl.BlockSpec((tm, D_MODEL), lambda i, f: (i, 0)),
        out_shape=jax.ShapeDtypeStruct((n, D_MODEL), F32),
        scratch_shapes=[pltpu.VMEM((tm, D_MODEL), BF16), pltpu.VMEM((tm, D_MODEL), F32)],
        compiler_params=_params(("parallel", "arbitrary")),
        name="dense_ffn",
    )(x, g, sc, sh, gt, gn, w_in, w_in, w_out)


def _rope_tables(seq):
    t = jnp.arange(seq, dtype=jnp.int32)
    row = (t // GRID_W).astype(F32)
    col = (t % GRID_W).astype(F32)
    inv = ROPE_THETA ** (-jnp.arange(ROPE_F, dtype=F32) / ROPE_F)
    ang_r = row[:, None] * inv
    ang_c = col[:, None] * inv
    cos = jnp.concatenate([jnp.cos(ang_r)] * 2 + [jnp.cos(ang_c)] * 2, axis=1)
    sin = jnp.concatenate([-jnp.sin(ang_r), jnp.sin(ang_r), -jnp.sin(ang_c), jnp.sin(ang_c)], axis=1)
    return jnp.tile(cos, (1, 2)), jnp.tile(sin, (1, 2))


def _prep_weights(w_in, g_qk, w_branch, w_o, w_ffn_in, w_ffn_out, w_router, b_router,
                  w_moe_in, w_moe_out):
    o_aq, o_ak, o_av, o_hq = 0, ATT_Q, ATT_Q + ATT_KV, ATT_Q + 2 * ATT_KV
    o_ga = o_hq + 5 * HG_F
    w_in_p = jnp.concatenate(
        [w_in[:, :, o_ga:], w_in[:, :, o_hq:o_ga], w_in[:, :, o_aq:o_hq]], axis=2).astype(BF16)
    pad_c = D_FF_PAD - D_FF
    w_fa = jnp.pad(w_ffn_in[:, :, :D_FF], ((0, 0), (0, 0), (0, pad_c)))
    w_fb = jnp.pad(w_ffn_in[:, :, D_FF:], ((0, 0), (0, 0), (0, pad_c)))
    w_ffn_in_p = jnp.concatenate([w_fa, w_fb], axis=2).astype(BF16)
    w_ffn_out_p = jnp.pad(w_ffn_out, ((0, 0), (0, pad_c), (0, 0))).astype(BF16)
    w_r_p = jnp.pad(w_router, ((0, 0), (0, 0), (0, LANES - N_EXPERTS)))
    b_r_p = jnp.pad(b_router, ((0, 0), (0, LANES - N_EXPERTS)))[:, None, :]
    g_q = jnp.tile(g_qk[:, 0, :], (1, LANES // HEAD_DIM))[:, None, :]
    g_k = jnp.tile(g_qk[:, 1, :], (1, LANES // HEAD_DIM))[:, None, :]
    return dict(w_in=w_in_p, w_branch=w_branch.astype(BF16), w_o=w_o.astype(BF16),
                w_ffn_in=w_ffn_in_p, w_ffn_out=w_ffn_out_p, w_r=w_r_p, b_r=b_r_p,
                w_moe_in=w_moe_in.astype(BF16), w_moe_out=w_moe_out.astype(BF16),
                g_q=g_q, g_k=g_k)


def _tile(seq, want):
    return min(seq, want)


def _trunk(x3, mods, g_norm, lb_logits, g_hgrn, wts):
    batch, seq, _ = x3.shape
    x = x3.reshape(batch * seq, D_MODEL)
    cos_t, sin_t = _rope_tables(seq)
    idx = np.arange(LANES)
    bd = jnp.asarray((idx[:, None] // HEAD_DIM == idx[None, :] // HEAD_DIM) / HEAD_DIM, BF16)
    tm_big = _tile(seq, 1024)
    tm = _tile(seq, 512)
    for l in range(DEPTH):
        mod = mods[l]
        sh1, sc1, gt1, sh2, sc2, gt2 = [m[:, None, :] for m in jnp.split(mod, 6, axis=-1)]
        gn = g_norm[l][:, None, :]
        proj = _inproj(x, gn[0], sc1, sh1, wts["w_in"][l], seq, tm_big)
        qh, ka, kb, vt = _rope(proj, cos_t, sin_t, wts["g_q"][l], wts["g_k"][l], bd, seq, tm)
        att = _attention(qh, ka, kb, vt, batch, seq, tm, tm)
        o_f = _hgrn(proj, lb_logits[0], batch, seq, tm, False, l)
        o_b = _hgrn(proj, lb_logits[1], batch, seq, tm, True, l)
        x = _mixout(att, o_f, o_b, proj, x, gt1, g_hgrn[l][None, :], gn[1],
                    wts["w_branch"][l, 0], wts["w_branch"][l, 1], wts["w_o"][l], seq, tm)
        if l % 2 == 0:
            x = _ffn(x, gn[2], sc2, sh2, gt2, gn[3], wts["w_ffn_in"][l // 2],
                     wts["w_ffn_out"][l // 2], seq, tm, FFN_CHUNK_DENSE)
        else:
            h, sel = _router(x, gn[2], sc2, sh2, wts["w_r"][l // 2], wts["b_r"][l // 2], seq, tm)
            moe_tile = min(MOE_TILE, 2 * batch * seq)
            src, texp, n_valid, pos_tiles = _route_plan(sel, moe_tile, tm)
            y_sorted = _moe_experts(h, src, texp, n_valid, wts["w_moe_in"][l // 2],
                                    wts["w_moe_out"][l // 2], moe_tile, FFN_CHUNK_EXPERT)
            x = _combine(pos_tiles, y_sorted, x, sel, gt2, gn[3], seq, tm)
    return x.reshape(batch, seq, D_MODEL)


def kernel(x_prompt, x_sample, c_prompt, c_sample, w_ada, b_ada, g_norm, w_in, g_qk, lb_logits,
           g_hgrn, w_branch, w_o, w_ffn_in, w_ffn_out, w_router, b_router, w_moe_in, w_moe_out):
    wts = _prep_weights(w_in, g_qk, w_branch, w_o, w_ffn_in, w_ffn_out, w_router, b_router,
                        w_moe_in, w_moe_out)
    n_p, n_s = c_prompt.shape[0], c_sample.shape[0]
    c_all = jnp.concatenate([c_prompt, c_sample], axis=0)
    pad = (-c_all.shape[0]) % SUBLANES
    c_all = jnp.pad(c_all, ((0, pad), (0, 0)))
    mods = _ada(c_all, w_ada, b_ada)
    y_p = _trunk(x_prompt, mods[:, :n_p], g_norm, lb_logits, g_hgrn, wts)
    y_s = _trunk(x_sample, mods[:, n_p:n_p + n_s], g_norm, lb_logits, g_hgrn, wts)
    return (y_p, y_s)
```

```python
import functools

import jax
import jax.numpy as jnp
import numpy as np
from jax import lax
from jax.experimental import pallas as pl
from jax.experimental.pallas import tpu as pltpu

F32 = jnp.float32
BF16 = jnp.bfloat16

D_MODEL = 1024
DEPTH = 4
GRID_W = 64
N_HEADS = 8
N_KV_HEADS = 2
HEAD_DIM = 64
KV_GROUP = N_HEADS // N_KV_HEADS
ROPE_F = HEAD_DIM // 4
ROPE_THETA = 10000.0
HG_HEADS = 4
HG_DK = 128
HG_DV = 128
ATT_Q = N_HEADS * HEAD_DIM
ATT_KV = N_KV_HEADS * HEAD_DIM
HG_F = HG_HEADS * HG_DK
D_FF = 2752
N_EXPERTS = 8
D_FF_EXPERT = 3584
EPS = 1e-6

LANES = 128
SUBLANES = 8
VMEM_LIMIT_BYTES = 56 * 1024 * 1024

COL_GA, COL_GB = 0, 1024
COL_HQ, COL_HFF, COL_HFB, COL_HI, COL_HG = 2048, 2560, 3072, 3584, 4096
COL_AQ, COL_AK, COL_AV = 4608, 5120, 5248
IN_COLS = 5376
IN_TILE_N = 1792

D_FF_PAD = 2816
FFN_CHUNK_DENSE = 1408
FFN_CHUNK_EXPERT = 1792
HG_CHUNK = 128
VT_ROWS = HEAD_DIM + 16
NRM_K_ROW = ATT_Q // LANES
SCORE_BOUND = 50.0
MOE_TILE = 512
SEL_E1, SEL_E2, SEL_W1, SEL_W2 = 0, 1, 2, 3


def _params(sem):
    return pltpu.CompilerParams(dimension_semantics=sem, vmem_limit_bytes=VMEM_LIMIT_BYTES)


def _sigmoid(x):
    return 1.0 / (1.0 + jnp.exp(-x))


def _silu(x):
    return x * _sigmoid(x)


def _rms(x, g):
    ms = jnp.mean(x * x, axis=-1, keepdims=True)
    return x * lax.rsqrt(ms + EPS) * g


def _ada_kernel(c_ref, w_ref, b_ref, o_ref):
    c = c_ref[...]
    o_ref[0] = jnp.dot(_silu(c), w_ref[0], preferred_element_type=F32,
                       precision=lax.Precision.HIGHEST) + b_ref[0]


def _ada(c, w_ada, b_ada):
    bp = c.shape[0]
    n_col = 6 * D_MODEL // D_MODEL
    return pl.pallas_call(
        _ada_kernel,
        grid=(DEPTH, n_col),
        in_specs=[
            pl.BlockSpec((bp, D_MODEL), lambda l, j: (0, 0)),
            pl.BlockSpec((1, D_MODEL, D_MODEL), lambda l, j: (l, 0, j)),
            pl.BlockSpec((1, 1, D_MODEL), lambda l, j: (l, 0, j)),
        ],
        out_specs=pl.BlockSpec((1, bp, D_MODEL), lambda l, j: (l, 0, j)),
        out_shape=jax.ShapeDtypeStruct((DEPTH, bp, 6 * D_MODEL), F32),
        compiler_params=_params(("arbitrary", "arbitrary")),
        name="ada_mod",
    )(c, w_ada, b_ada.reshape(DEPTH, 1, 6 * D_MODEL))


def _inproj_kernel(x_ref, g_ref, sc_ref, sh_ref, w_ref, o_ref, h_scr):
    @pl.when(pl.program_id(1) == 0)
    def _():
        y = _rms(x_ref[...], g_ref[...])
        h_scr[...] = (y * (1.0 + sc_ref[0]) + sh_ref[0]).astype(BF16)

    o_ref[...] = jnp.dot(h_scr[...], w_ref[...], preferred_element_type=F32).astype(BF16)


def _inproj(x, g, sc, sh, w, seq, tm):
    n = x.shape[0]
    per_b = seq // tm
    return pl.pallas_call(
        _inproj_kernel,
        grid=(n // tm, IN_COLS // IN_TILE_N),
        in_specs=[
            pl.BlockSpec((tm, D_MODEL), lambda i, j: (i, 0)),
            pl.BlockSpec((1, D_MODEL), lambda i, j: (0, 0)),
            pl.BlockSpec((1, 1, D_MODEL), lambda i, j: (i // per_b, 0, 0)),
            pl.BlockSpec((1, 1, D_MODEL), lambda i, j: (i // per_b, 0, 0)),
            pl.BlockSpec((D_MODEL, IN_TILE_N), lambda i, j: (0, j)),
        ],
        out_specs=pl.BlockSpec((tm, IN_TILE_N), lambda i, j: (i, j)),
        out_shape=jax.ShapeDtypeStruct((n, IN_COLS), BF16),
        scratch_shapes=[pltpu.VMEM((tm, D_MODEL), BF16)],
        compiler_params=_params(("parallel", "arbitrary")),
        name="in_proj",
    )(x, g, sc, sh, w)


def _rope_kernel(aq_ref, ak_ref, av_ref, cos_ref, sin_ref, gq_ref, gk_ref, bd_ref,
                 qh_ref, ka_ref, kb_ref, vt_ref, nrm_ref):
    tm = aq_ref.shape[0]
    lane = lax.broadcasted_iota(jnp.int32, (tm, LANES), 1)
    first_half = (lane % (2 * ROPE_F)) < ROPE_F
    low = lane < HEAD_DIM
    cos = cos_ref[...]
    sin = sin_ref[...]
    bd = bd_ref[...]

    def norm_rope(x, g):
        xx = x * x
        hi = xx.astype(BF16)
        lo = (xx - hi.astype(F32)).astype(BF16)
        ms = (jnp.dot(hi, bd, preferred_element_type=F32)
              + jnp.dot(lo, bd, preferred_element_type=F32))
        y = x * lax.rsqrt(ms + EPS) * g
        nxt = pltpu.roll(y, LANES - ROPE_F, 1)
        prv = pltpu.roll(y, ROPE_F, 1)
        return y * cos + jnp.where(first_half, nxt, prv) * sin

    def max_sq_norm(y):
        ss = jnp.dot((y * y).astype(BF16), bd, preferred_element_type=F32) * HEAD_DIM
        return jnp.max(ss, axis=0, keepdims=True)

    gq = gq_ref[...]
    scale = HEAD_DIM ** -0.5 * float(np.log2(np.e))
    nrm_ref[...] = jnp.zeros(nrm_ref.shape, F32)
    for j in range(ATT_Q // LANES):
        r = norm_rope(aq_ref[:, j * LANES:(j + 1) * LANES].astype(F32), gq) * scale
        qh_ref[:, (2 * j) * LANES:(2 * j + 1) * LANES] = jnp.where(low, r, 0.0).astype(BF16)
        qh_ref[:, (2 * j + 1) * LANES:(2 * j + 2) * LANES] = jnp.where(low, 0.0, r).astype(BF16)
        nrm_ref[0, j:j + 1, :] = max_sq_norm(r)

    k = norm_rope(ak_ref[...].astype(F32), gk_ref[...])
    nrm_ref[0, NRM_K_ROW:NRM_K_ROW + 1, :] = max_sq_norm(k)
    ka_ref[...] = k.astype(BF16)
    kb_ref[...] = pltpu.roll(k, HEAD_DIM, 1).astype(BF16)
    vt = av_ref[...].astype(F32).T.astype(BF16)
    ones = jnp.ones((VT_ROWS - HEAD_DIM, tm), BF16)
    for g in range(N_KV_HEADS):
        vt_ref[g, 0:HEAD_DIM, :] = vt[g * HEAD_DIM:(g + 1) * HEAD_DIM, :]
        vt_ref[g, HEAD_DIM:VT_ROWS, :] = ones


def _rope(proj, cos_t, sin_t, gq, gk, bd, seq, tm):
    n = proj.shape[0]
    per_b = seq // tm
    kv_spec = pl.BlockSpec((tm, LANES), lambda i: (i, 0))
    vt_spec = pl.BlockSpec((N_KV_HEADS, VT_ROWS, tm), lambda i: (0, 0, i))
    return pl.pallas_call(
        _rope_kernel,
        grid=(n // tm,),
        in_specs=[
            pl.BlockSpec((tm, ATT_Q), lambda i: (i, COL_AQ // ATT_Q)),
            pl.BlockSpec((tm, LANES), lambda i: (i, COL_AK // LANES)),
            pl.BlockSpec((tm, LANES), lambda i: (i, COL_AV // LANES)),
            pl.BlockSpec((tm, LANES), lambda i: (i % per_b, 0)),
            pl.BlockSpec((tm, LANES), lambda i: (i % per_b, 0)),
            pl.BlockSpec((1, LANES), lambda i: (0, 0)),
            pl.BlockSpec((1, LANES), lambda i: (0, 0)),
            pl.BlockSpec((LANES, LANES), lambda i: (0, 0)),
        ],
        out_specs=[pl.BlockSpec((tm, N_HEADS * LANES), lambda i: (i, 0)),
                   kv_spec, kv_spec, vt_spec,
                   pl.BlockSpec((1, SUBLANES, LANES), lambda i: (i, 0, 0))],
        out_shape=[jax.ShapeDtypeStruct((n, N_HEADS * LANES), BF16)]
        + [jax.ShapeDtypeStruct((n, LANES), BF16)] * 2
        + [jax.ShapeDtypeStruct((N_KV_HEADS, VT_ROWS, n), BF16),
           jax.ShapeDtypeStruct((n // tm, SUBLANES, LANES), F32)],
        compiler_params=_params(("parallel",)),
        name="qk_norm_rope",
    )(proj, proj, proj, cos_t, sin_t, gq, gk, bd)


def _attn_kernel(bounded_ref, q_ref, ka_ref, kb_ref, vt_ref, o_ref, m_scr, acc_scr):
    j = pl.program_id(2)
    step = (pl.program_id(0) * pl.num_programs(1) + pl.program_id(1)) * pl.num_programs(2) + j

    @pl.when(j == 0)
    def _():
        m_scr[...] = jnp.full(m_scr.shape, -jnp.inf, F32)
        acc_scr[...] = jnp.zeros(acc_scr.shape, F32)

    def scores(h):
        g, half = h // KV_GROUP, h % 2
        k = (ka_ref if g == half else kb_ref)[...]
        q = q_ref[:, h * LANES:(h + 1) * LANES]
        return lax.dot_general(k, q, (((1,), (1,)), ((), ())), preferred_element_type=F32)

    def all_heads(tile_max):
        st_next = scores(0)
        for h in range(N_HEADS):
            vt = vt_ref[h // KV_GROUP]
            st = st_next
            if h + 1 < N_HEADS:
                st_next = scores(h + 1)
            m_prev = m_scr[h]
            m_new = jnp.maximum(m_prev, tile_max(st))
            alpha = jnp.exp2(m_prev - m_new)
            pt = jnp.exp2((st - m_new).astype(BF16))
            acc_scr[h] = alpha * acc_scr[h] + jnp.dot(vt, pt, preferred_element_type=F32)
            m_scr[h] = m_new

    @pl.when(bounded_ref[step] != 0)
    def _():
        all_heads(lambda st: 0.0)

    @pl.when(bounded_ref[step] == 0)
    def _():
        all_heads(lambda st: jnp.max(st, axis=0, keepdims=True))

    @pl.when(j == pl.num_programs(2) - 1)
    def _():
        for jj in range(N_HEADS // 2):
            pair = []
            for h in (2 * jj, 2 * jj + 1):
                acc = acc_scr[h]
                pair.append(acc[0:HEAD_DIM, :] / acc[HEAD_DIM:HEAD_DIM + 1, :])
            o_ref[:, jj * LANES:(jj + 1) * LANES] = jnp.concatenate(pair, axis=0).T.astype(BF16)


def _attention(qh, ka, kb, vt, nrm, batch, seq, tq, tk):
    n = qh.shape[0]
    nq, nk = seq // tq, seq // tk
    halves = jnp.array([0, HEAD_DIM])
    qn2 = nrm[:, :NRM_K_ROW, :][:, :, halves].reshape(batch, nq, 1, N_HEADS)
    kn2 = jnp.repeat(nrm[:, NRM_K_ROW, :][:, halves], KV_GROUP, axis=1).reshape(batch, 1, nk, N_HEADS)
    bounded = jnp.all(qn2 * kn2 <= SCORE_BOUND ** 2, axis=-1).astype(jnp.int32).reshape(-1)
    k_spec = pl.BlockSpec((tk, LANES), lambda b, i, j, fl: (b * nk + j, 0))
    vt_spec = pl.BlockSpec((N_KV_HEADS, VT_ROWS, tk), lambda b, i, j, fl: (0, 0, b * nk + j))
    grid_spec = pltpu.PrefetchScalarGridSpec(
        num_scalar_prefetch=1,
        grid=(batch, nq, nk),
        in_specs=[pl.BlockSpec((tq, N_HEADS * LANES), lambda b, i, j, fl: (b * nq + i, 0)),
                  k_spec, k_spec, vt_spec],
        out_specs=pl.BlockSpec((tq, ATT_Q), lambda b, i, j, fl: (b * nq + i, 0)),
        scratch_shapes=[pltpu.VMEM((N_HEADS, 1, tq), F32),
                        pltpu.VMEM((N_HEADS, VT_ROWS, tq), F32)],
    )
    return pl.pallas_call(
        _attn_kernel,
        grid_spec=grid_spec,
        out_shape=jax.ShapeDtypeStruct((n, ATT_Q), BF16),
        compiler_params=_params(("parallel", "parallel", "arbitrary")),
        name="gqa_attention",
    )(bounded, qh, ka, kb, vt)


def _level_ref(b, m, rev):
    c = b.shape[0]
    g = 2 * m
    if g >= SUBLANES:
        b3 = b.reshape(c // g, g, LANES)
        r = m if rev else m - 1
        return jnp.broadcast_to(b3[:, r:r + 1, :], b3.shape).reshape(c, LANES)
    b3 = b.reshape(c // SUBLANES, SUBLANES, LANES)
    sub = lax.broadcasted_iota(jnp.int32, b3.shape, 1)
    out = None
    for p0 in range(0, SUBLANES, g):
        r = p0 + (m if rev else m - 1)
        cand = jnp.broadcast_to(b3[:, r:r + 1, :], b3.shape)
        out = cand if out is None else jnp.where(sub >= p0, cand, out)
    return out.reshape(c, LANES)


def _chunk_cumsum(x, tri):
    hi = x.astype(BF16)
    r1 = x - hi.astype(F32)
    mid = r1.astype(BF16)
    lo = (r1 - mid.astype(F32)).astype(BF16)
    return (jnp.dot(tri, hi, preferred_element_type=F32)
            + jnp.dot(tri, mid, preferred_element_type=F32)
            + jnp.dot(tri, lo, preferred_element_type=F32))


def _neg_abs(z):
    bits = lax.bitcast_convert_type(z, jnp.int32) | jnp.int32(-2 ** 31)
    return lax.bitcast_convert_type(bits, F32)


def _hgrn_kernel(hq_ref, hf_ref, hi_ref, lbl_ref, o_ref, st_scr, *, rev, layer, chunk):
    c = chunk
    n_chunks = hq_ref.shape[0] // c

    @pl.when(pl.program_id(1) == 0)
    def _():
        st_scr[...] = jnp.zeros(st_scr.shape, F32)

    lbl = lbl_ref[...]
    lbe = jnp.exp(lbl - jnp.max(lbl, axis=0, keepdims=True))
    lbp = lbe / jnp.sum(lbe, axis=0, keepdims=True)
    lb_all = jnp.zeros((1, HG_F), F32)
    for l in range(1, layer + 1):
        lb_all = lb_all + lbp[l:l + 1, :]

    row = lax.broadcasted_iota(jnp.int32, (c, LANES), 0)
    row_a = lax.broadcasted_iota(jnp.int32, (c, c), 0)
    col_a = lax.broadcasted_iota(jnp.int32, (c, c), 1)
    tri = jnp.where((row_a <= col_a) if rev else (row_a >= col_a), 1.0, 0.0).astype(BF16)
    lev = 31 - lax.clz(row_a ^ col_a)
    lev = jnp.where((row_a < col_a) if rev else (row_a > col_a), lev, -1)
    q_scale = HG_DK ** -0.5
    log2e = float(np.log2(np.e))
    nt_dims = (((1,), (1,)), ((), ()))

    def body(ci, carry):
        cc = (n_chunks - 1 - ci) if rev else ci
        r0 = pl.multiple_of(cc * c, c)
        heads = range(HG_HEADS)
        cols = [slice(h * LANES, (h + 1) * LANES) for h in heads]
        v = [hi_ref[pl.ds(r0, c), cols[h]] for h in heads]
        q = [_silu(hq_ref[pl.ds(r0, c), cols[h]].astype(F32)) * q_scale for h in heads]
        f = [lb_all[:, cols[h]] + (1.0 - lb_all[:, cols[h]])
             * _sigmoid(hf_ref[pl.ds(r0, c), cols[h]].astype(F32)) for h in heads]
        k = [1.0 - f[h] for h in heads]
        b = [_chunk_cumsum(jnp.log(f[h]) * log2e, tri) for h in heads]
        a = [jnp.zeros((c, c), F32) for h in heads]
        m = c // 2
        while m >= 1:
            upper = (row & m) != 0
            for h in heads:
                e = jnp.exp2(_neg_abs(b[h] - _level_ref(b[h], m, rev)).astype(BF16))
                w = jnp.where(upper, k[h], q[h]) if rev else jnp.where(upper, q[h], k[h])
                w = w.astype(BF16) * e
                al = lax.dot_general(w, w, nt_dims, preferred_element_type=F32)
                a[h] = jnp.where(lev == (m.bit_length() - 1), al, a[h])
            m //= 2
        for h in heads:
            st = st_scr[h]
            o = lax.dot_general((q[h] * jnp.exp2(b[h])).astype(BF16), st.astype(BF16), nt_dims,
                                preferred_element_type=F32)
            diag = jnp.sum(q[h] * k[h], axis=-1, keepdims=True)
            o = (o + jnp.dot(a[h].astype(BF16), v[h], preferred_element_type=F32)
                 + diag * v[h].astype(F32))
            o_ref[pl.ds(r0, c), cols[h]] = o.astype(o_ref.dtype)
            b_end = b[h][0:1, :] if rev else b[h][c - 1:c, :]
            k_end = (k[h] * jnp.exp2(b_end - b[h])).astype(BF16)
            upd = lax.dot_general(v[h], k_end, (((0,), (0,)), ((), ())), preferred_element_type=F32)
            st_scr[h] = st * jnp.exp2(b_end) + upd
        return carry

    lax.fori_loop(0, n_chunks, body, 0, unroll=True)


def _hgrn(proj, lb_logits_dir, batch, seq, tt, rev, layer):
    n = proj.shape[0]
    nt = seq // tt
    col_f = (COL_HFB if rev else COL_HFF) // HG_F

    def rows(b, i):
        return b * nt + ((nt - 1 - i) if rev else i)

    return pl.pallas_call(
        functools.partial(_hgrn_kernel, rev=rev, layer=layer, chunk=min(HG_CHUNK, tt)),
        grid=(batch, nt),
        in_specs=[
            pl.BlockSpec((tt, HG_F), lambda b, i: (rows(b, i), COL_HQ // HG_F)),
            pl.BlockSpec((tt, HG_F), lambda b, i: (rows(b, i), col_f)),
            pl.BlockSpec((tt, HG_F), lambda b, i: (rows(b, i), COL_HI // HG_F)),
            pl.BlockSpec((DEPTH, HG_F), lambda b, i: (0, 0)),
        ],
        out_specs=pl.BlockSpec((tt, HG_F), lambda b, i: (rows(b, i), 0)),
        out_shape=jax.ShapeDtypeStruct((n, HG_F), BF16),
        scratch_shapes=[pltpu.VMEM((HG_HEADS, HG_DV, HG_DK), F32)],
        compiler_params=_params(("parallel", "arbitrary")),
        name="hgrn2_bwd" if rev else "hgrn2_fwd",
    )(proj, proj, proj, lb_logits_dir)


def _mixout_kernel(att_ref, of_ref, ob_ref, hg_ref, ga_ref, gb_ref, x_ref, gt_ref, ghg_ref,
                   gn_ref, wa_ref, wb_ref, wo_ref, o_ref):
    ghg = ghg_ref[...]
    recs = []
    for h in range(HG_HEADS):
        cols = slice(h * LANES, (h + 1) * LANES)
        o = of_ref[:, cols].astype(F32) + ob_ref[:, cols].astype(F32)
        recs.append((_rms(o, ghg) * _silu(hg_ref[:, cols].astype(F32))).astype(BF16))
    rec = jnp.concatenate(recs, axis=1)
    pa = jnp.dot(att_ref[...], wa_ref[...], preferred_element_type=F32)
    pb = jnp.dot(rec, wb_ref[...], preferred_element_type=F32)
    merged = (_sigmoid(ga_ref[...].astype(F32)) * pa + _sigmoid(gb_ref[...].astype(F32)) * pb)
    y = jnp.dot(merged.astype(BF16), wo_ref[...], preferred_element_type=F32)
    o_ref[...] = x_ref[...] + gt_ref[0] * _rms(y, gn_ref[...])


def _mixout(att, o_f, o_b, proj, x, gt, g_hgrn, g_norm, w_a, w_b, w_o, seq, tm):
    n = x.shape[0]
    per_b = seq // tm
    half = pl.BlockSpec((tm, ATT_Q), lambda i: (i, 0))
    const = lambda shape: pl.BlockSpec(shape, lambda i: (0,) * len(shape))
    return pl.pallas_call(
        _mixout_kernel,
        grid=(n // tm,),
        in_specs=[
            half, half, half,
            pl.BlockSpec((tm, HG_F), lambda i: (i, COL_HG // HG_F)),
            pl.BlockSpec((tm, D_MODEL), lambda i: (i, COL_GA // D_MODEL)),
            pl.BlockSpec((tm, D_MODEL), lambda i: (i, COL_GB // D_MODEL)),
            pl.BlockSpec((tm, D_MODEL), lambda i: (i, 0)),
            pl.BlockSpec((1, 1, D_MODEL), lambda i: (i // per_b, 0, 0)),
            const((1, LANES)), const((1, D_MODEL)),
            const((ATT_Q, D_MODEL)), const((HG_F, D_MODEL)), const((D_MODEL, D_MODEL)),
        ],
        out_specs=pl.BlockSpec((tm, D_MODEL), lambda i: (i, 0)),
        out_shape=jax.ShapeDtypeStruct((n, D_MODEL), F32),
        compiler_params=_params(("parallel",)),
        name="mixer_out",
    )(att, o_f, o_b, proj, proj, proj, x, gt, g_hgrn, g_norm, w_a, w_b, w_o)


def _router_kernel(x_ref, g_ref, sc_ref, sh_ref, wr_ref, br_ref, h_ref, sel_ref):
    h = _rms(x_ref[...], g_ref[...]) * (1.0 + sc_ref[0]) + sh_ref[0]
    tm = h.shape[0]
    for j in range(D_MODEL // LANES):
        h_ref[pl.ds(j, tm, stride=SUBLANES), :] = h[:, j * LANES:(j + 1) * LANES]
    logits = jnp.dot(h, wr_ref[...], preferred_element_type=F32,
                     precision=lax.Precision.HIGHEST) + br_ref[...]
    lane = lax.broadcasted_iota(jnp.int32, logits.shape, 1).astype(F32)
    logits = jnp.where(lane < N_EXPERTS, logits, -jnp.inf)
    m1 = jnp.max(logits, axis=-1, keepdims=True)
    i1 = jnp.min(jnp.where(logits == m1, lane, float(LANES)), axis=-1, keepdims=True)
    rest = jnp.where(lane == i1, -jnp.inf, logits)
    m2 = jnp.max(rest, axis=-1, keepdims=True)
    i2 = jnp.min(jnp.where(rest == m2, lane, float(LANES)), axis=-1, keepdims=True)
    e2 = jnp.exp(m2 - m1)
    w1 = 1.0 / (1.0 + e2)
    w2 = e2 / (1.0 + e2)
    sel_ref[...] = (jnp.where(lane == SEL_E1, i1, 0.0) + jnp.where(lane == SEL_E2, i2, 0.0)
                    + jnp.where(lane == SEL_W1, w1, 0.0) + jnp.where(lane == SEL_W2, w2, 0.0))


def _router(x, g, sc, sh, w_r, b_r, seq, tm):
    n = x.shape[0]
    per_b = seq // tm
    return pl.pallas_call(
        _router_kernel,
        grid=(n // tm,),
        in_specs=[
            pl.BlockSpec((tm, D_MODEL), lambda i: (i, 0)),
            pl.BlockSpec((1, D_MODEL), lambda i: (0, 0)),
            pl.BlockSpec((1, 1, D_MODEL), lambda i: (i // per_b, 0, 0)),
            pl.BlockSpec((1, 1, D_MODEL), lambda i: (i // per_b, 0, 0)),
            pl.BlockSpec((D_MODEL, LANES), lambda i: (0, 0)),
            pl.BlockSpec((1, LANES), lambda i: (0, 0)),
        ],
        out_specs=[pl.BlockSpec((tm * SUBLANES, LANES), lambda i: (i, 0)),
                   pl.BlockSpec((tm, LANES), lambda i: (i, 0))],
        out_shape=[jax.ShapeDtypeStruct((n * SUBLANES, LANES), F32),
                   jax.ShapeDtypeStruct((n, LANES), F32)],
        compiler_params=_params(("parallel",)),
        name="router_top2",
    )(x, g, sc, sh, w_r, b_r)


def _route_plan(sel, tm, tmc):
    n = sel.shape[0]
    r = 2 * n
    flat_e = sel[:, :2].astype(jnp.int32).reshape(-1)
    onehot = (flat_e[:, None] == jnp.arange(N_EXPERTS, dtype=jnp.int32)[None, :]).astype(jnp.int32)
    csum = jnp.cumsum(onehot, axis=0)
    rank = jnp.sum(csum * onehot, axis=1) - 1
    counts = csum[-1]
    padded = (counts + tm - 1) // tm * tm
    ends = jnp.cumsum(padded)
    pos = (ends - padded)[flat_e] + rank
    n_tiles = r // tm + N_EXPERTS
    src = jnp.zeros((n_tiles * tm,), jnp.int32).at[pos].set(jnp.arange(r, dtype=jnp.int32) // 2)
    src = src * SUBLANES
    pos = pos * SUBLANES
    n_valid = (ends[-1] // tm).astype(jnp.int32)
    tile_start = jnp.arange(n_tiles, dtype=jnp.int32) * tm
    texp = jnp.minimum(jnp.searchsorted(ends, tile_start, side="right"), N_EXPERTS - 1)
    texp = jnp.where(tile_start < ends[-1], texp, texp[jnp.maximum(n_valid - 1, 0)])
    pos_tiles = pos.reshape(n // tmc, tmc, 2).transpose(0, 2, 1).reshape(n // tmc, 2 * tmc)
    return src.reshape(n_tiles, tm), texp.astype(jnp.int32), n_valid.reshape(1), pos_tiles


def _moe_kernel(texp_ref, nv_ref, src_hbm, h_hbm, wa_ref, wb_ref, wo_ref, y_ref,
                hbuf, hb16, acc, idx_smem, gsem, isem, *, tm):
    t = pl.program_id(0)
    f = pl.program_id(1)
    nv = nv_ref[0]
    slot = t % 2
    other = 1 - slot
    n_lane_blocks = D_MODEL // LANES

    def idx_copy(tile, s):
        return pltpu.make_async_copy(src_hbm.at[pl.ds(tile, 1)], idx_smem.at[pl.ds(s, 1)], isem.at[s])

    def issue_row(s, r):
        src_row = pl.multiple_of(idx_smem[s, r], SUBLANES)
        dst_row = pl.multiple_of(r * SUBLANES, SUBLANES)
        pltpu.make_async_copy(h_hbm.at[pl.ds(src_row, SUBLANES)],
                              hbuf.at[s, pl.ds(dst_row, SUBLANES)], gsem.at[s]).start()

    def wait_rows(s):
        pltpu.make_async_copy(h_hbm.at[pl.ds(0, tm * SUBLANES)], hbuf.at[s], gsem.at[s]).wait()

    def swiglu_chunk():
        h = hb16[...]
        a = jnp.dot(h, wa_ref[0], preferred_element_type=F32)
        b = jnp.dot(h, wb_ref[0], preferred_element_type=F32)
        return jnp.dot((_silu(a) * b).astype(BF16), wo_ref[0], preferred_element_type=F32)

    @pl.when((t == 0) & (f == 0))
    def _():
        first = idx_copy(0, 0)
        first.start()
        first.wait()
        lax.fori_loop(0, tm, lambda r, c: (issue_row(0, r), c)[1], 0, unroll=8)

    @pl.when((f == 0) & (t <= nv))
    def _():
        wait_rows(slot)
        for j in range(n_lane_blocks):
            hb16[:, j * LANES:(j + 1) * LANES] = (
                hbuf[slot, pl.ds(j, tm, stride=SUBLANES), :].astype(BF16))

    @pl.when((f == 0) & (t < nv))
    def _():
        idx_copy(t + 1, other).start()
        acc[...] = swiglu_chunk()

    @pl.when((f == 1) & (t < nv))
    def _():
        idx_copy(t + 1, other).wait()
        for r in range(tm):
            issue_row(other, r)
        y = swiglu_chunk()
        for j in range(n_lane_blocks):
            cols = slice(j * LANES, (j + 1) * LANES)
            y_ref[pl.ds(j, tm, stride=SUBLANES), :] = acc[:, cols] + y[:, cols]

    @pl.when((t >= nv) & (f == 0))
    def _():
        y_ref[...] = jnp.zeros(y_ref.shape, F32)


def _moe_experts(h, src, texp, n_valid, w_in, w_out, tm, chunk):
    n_tiles = src.shape[0]
    ff = w_in.shape[2] // 2
    n_fc = ff // chunk
    assert n_fc == 2, "the kernel's gather schedule and accumulation are written for two chunks"

    def fidx(t, f, nv):
        return jnp.where(t < nv[0], f, n_fc - 1)

    grid_spec = pltpu.PrefetchScalarGridSpec(
        num_scalar_prefetch=2,
        grid=(n_tiles, n_fc),
        in_specs=[
            pl.BlockSpec(memory_space=pl.ANY),
            pl.BlockSpec(memory_space=pl.ANY),
            pl.BlockSpec((1, D_MODEL, chunk), lambda t, f, te, nv: (te[t], 0, fidx(t, f, nv))),
            pl.BlockSpec((1, D_MODEL, chunk), lambda t, f, te, nv: (te[t], 0, n_fc + fidx(t, f, nv))),
            pl.BlockSpec((1, chunk, D_MODEL), lambda t, f, te, nv: (te[t], fidx(t, f, nv), 0)),
        ],
        out_specs=pl.BlockSpec((tm * SUBLANES, LANES), lambda t, f, te, nv: (t, 0)),
        scratch_shapes=[
            pltpu.VMEM((2, tm * SUBLANES, LANES), F32),
            pltpu.VMEM((tm, D_MODEL), BF16),
            pltpu.VMEM((tm, D_MODEL), F32),
            pltpu.SMEM((2, tm), jnp.int32),
            pltpu.SemaphoreType.DMA((2,)),
            pltpu.SemaphoreType.DMA((2,)),
        ],
    )
    return pl.pallas_call(
        functools.partial(_moe_kernel, tm=tm),
        grid_spec=grid_spec,
        out_shape=jax.ShapeDtypeStruct((n_tiles * tm * SUBLANES, LANES), F32),
        compiler_params=_params(("arbitrary", "arbitrary")),
        name="moe_experts",
    )(texp, n_valid, src, h, w_in, w_in, w_out)


def _combine_kernel(pos_hbm, y_hbm, x_ref, sel_ref, gt_ref, gn_ref, o_ref, ybuf, idx_smem, gsem, isem,
                    *, tmc):
    i = pl.program_id(0)
    fetch = pltpu.make_async_copy(pos_hbm.at[pl.ds(i, 1)], idx_smem, isem)
    fetch.start()
    fetch.wait()

    def body(r, carry):
        src_row = pl.multiple_of(idx_smem[0, r], SUBLANES)
        dst_row = pl.multiple_of(r * SUBLANES, SUBLANES)
        pltpu.make_async_copy(y_hbm.at[pl.ds(src_row, SUBLANES)],
                              ybuf.at[pl.ds(dst_row, SUBLANES)], gsem).start()
        return carry
    lax.fori_loop(0, 2 * tmc, body, 0, unroll=8)
    pltpu.make_async_copy(y_hbm.at[pl.ds(0, 2 * tmc * SUBLANES)], ybuf, gsem).wait()

    sel = sel_ref[...]
    w1 = sel[:, SEL_W1:SEL_W1 + 1]
    w2 = sel[:, SEL_W2:SEL_W2 + 1]
    second = tmc * SUBLANES
    y = jnp.concatenate(
        [w1 * ybuf[pl.ds(j, tmc, stride=SUBLANES), :]
         + w2 * ybuf[pl.ds(second + j, tmc, stride=SUBLANES), :]
         for j in range(D_MODEL // LANES)], axis=1)
    o_ref[...] = x_ref[...] + gt_ref[0] * _rms(y, gn_ref[...])


def _combine(pos_tiles, y_sorted, x, sel, gt, gn, seq, tmc):
    n = x.shape[0]
    per_b = seq // tmc
    return pl.pallas_call(
        functools.partial(_combine_kernel, tmc=tmc),
        grid=(n // tmc,),
        in_specs=[
            pl.BlockSpec(memory_space=pl.ANY),
            pl.BlockSpec(memory_space=pl.ANY),
            pl.BlockSpec((tmc, D_MODEL), lambda i: (i, 0)),
            pl.BlockSpec((tmc, LANES), lambda i: (i, 0)),
            pl.BlockSpec((1, 1, D_MODEL), lambda i: (i // per_b, 0, 0)),
            pl.BlockSpec((1, D_MODEL), lambda i: (0, 0)),
        ],
        out_specs=pl.BlockSpec((tmc, D_MODEL), lambda i: (i, 0)),
        out_shape=jax.ShapeDtypeStruct((n, D_MODEL), F32),
        scratch_shapes=[
            pltpu.VMEM((2 * tmc * SUBLANES, LANES), F32),
            pltpu.SMEM((1, 2 * tmc), jnp.int32),
            pltpu.SemaphoreType.DMA,
            pltpu.SemaphoreType.DMA,
        ],
        compiler_params=_params(("arbitrary",)),
        name="moe_combine",
    )(pos_tiles, y_sorted, x, sel, gt, gn)


def _ffn_kernel(x_ref, g_ref, sc_ref, sh_ref, gt_ref, gn_ref, wa_ref, wb_ref, wo_ref,
                o_ref, h_scr, acc_scr):
    f = pl.program_id(1)

    @pl.when(f == 0)
    def _():
        y = _rms(x_ref[...], g_ref[...])
        h_scr[...] = (y * (1.0 + sc_ref[0]) + sh_ref[0]).astype(BF16)
        acc_scr[...] = jnp.zeros(acc_scr.shape, F32)

    h = h_scr[...]
    a = jnp.dot(h, wa_ref[...], preferred_element_type=F32)
    b = jnp.dot(h, wb_ref[...], preferred_element_type=F32)
    acc_scr[...] += jnp.dot((_silu(a) * b).astype(BF16), wo_ref[...], preferred_element_type=F32)

    @pl.when(f == pl.num_programs(1) - 1)
    def _():
        o_ref[...] = x_ref[...] + gt_ref[0] * _rms(acc_scr[...], gn_ref[...])


def _ffn(x, g, sc, sh, gt, gn, w_in, w_out, seq, tm, chunk):
    n = x.shape[0]
    per_b = seq // tm
    n_fc = w_in.shape[1] // 2 // chunk
    vec = pl.BlockSpec((1, D_MODEL), lambda i, f: (0, 0))
    mod = pl.BlockSpec((1, 1, D_MODEL), lambda i, f: (i // per_b, 0, 0))
    return pl.pallas_call(
        _ffn_kernel,
        grid=(n // tm, n_fc),
        in_specs=[
            pl.BlockSpec((tm, D_MODEL), lambda i, f: (i, 0)), vec, mod, mod, mod, vec,
            pl.BlockSpec((D_MODEL, chunk), lambda i, f: (0, f)),
            pl.BlockSpec((D_MODEL, chunk), lambda i, f: (0, n_fc + f)),
            pl.BlockSpec((chunk, D_MODEL), lambda i, f: (f, 0)),
        ],
        out_specs=pl.BlockSpec((tm, D_MODEL), lambda i, f: (i, 0)),
        out_shape=jax.ShapeDtypeStruct((n, D_MODEL), F32),
        scratch_shapes=[pltpu.VMEM((tm, D_MODEL), BF16), pltpu.VMEM((tm, D_MODEL), F32)],
        compiler_params=_params(("parallel", "arbitrary")),
        name="dense_ffn",
    )(x, g, sc, sh, gt, gn, w_in, w_in, w_out)


def _rope_tables(seq):
    t = jnp.arange(seq, dtype=jnp.int32)
    row = (t // GRID_W).astype(F32)
    col = (t % GRID_W).astype(F32)
    inv = ROPE_THETA ** (-jnp.arange(ROPE_F, dtype=F32) / ROPE_F)
    ang_r = row[:, None] * inv
    ang_c = col[:, None] * inv
    cos = jnp.concatenate([jnp.cos(ang_r)] * 2 + [jnp.cos(ang_c)] * 2, axis=1)
    sin = jnp.concatenate([-jnp.sin(ang_r), jnp.sin(ang_r), -jnp.sin(ang_c), jnp.sin(ang_c)], axis=1)
    return jnp.tile(cos, (1, 2)), jnp.tile(sin, (1, 2))


def _prep_weights(w_in, g_qk, w_branch, w_o, w_ffn_in, w_ffn_out, w_router, b_router,
                  w_moe_in, w_moe_out):
    o_aq, o_ak, o_av, o_hq = 0, ATT_Q, ATT_Q + ATT_KV, ATT_Q + 2 * ATT_KV
    o_ga = o_hq + 5 * HG_F
    w_in_p = jnp.concatenate(
        [w_in[:, :, o_ga:], w_in[:, :, o_hq:o_ga], w_in[:, :, o_aq:o_hq]], axis=2).astype(BF16)
    pad_c = D_FF_PAD - D_FF
    w_fa = jnp.pad(w_ffn_in[:, :, :D_FF], ((0, 0), (0, 0), (0, pad_c)))
    w_fb = jnp.pad(w_ffn_in[:, :, D_FF:], ((0, 0), (0, 0), (0, pad_c)))
    w_ffn_in_p = jnp.concatenate([w_fa, w_fb], axis=2).astype(BF16)
    w_ffn_out_p = jnp.pad(w_ffn_out, ((0, 0), (0, pad_c), (0, 0))).astype(BF16)
    w_r_p = jnp.pad(w_router, ((0, 0), (0, 0), (0, LANES - N_EXPERTS)))
    b_r_p = jnp.pad(b_router, ((0, 0), (0, LANES - N_EXPERTS)))[:, None, :]
    g_q = jnp.tile(g_qk[:, 0, :], (1, LANES // HEAD_DIM))[:, None, :]
    g_k = jnp.tile(g_qk[:, 1, :], (1, LANES // HEAD_DIM))[:, None, :]
    return dict(w_in=w_in_p, w_branch=w_branch.astype(BF16), w_o=w_o.astype(BF16),
                w_ffn_in=w_ffn_in_p, w_ffn_out=w_ffn_out_p, w_r=w_r_p, b_r=b_r_p,
                w_moe_in=w_moe_in.astype(BF16), w_moe_out=w_moe_out.astype(BF16),
                g_q=g_q, g_k=g_k)


def _tile(seq, want):
    return min(seq, want)


def _trunk(x3, mods, g_norm, lb_logits, g_hgrn, wts):
    batch, seq, _ = x3.shape
    x = x3.reshape(batch * seq, D_MODEL)
    cos_t, sin_t = _rope_tables(seq)
    idx = np.arange(LANES)
    bd = jnp.asarray((idx[:, None] // HEAD_DIM == idx[None, :] // HEAD_DIM) / HEAD_DIM, BF16)
    tm_big = _tile(seq, 1024)
    tm = _tile(seq, 512)
    for l in range(DEPTH):
        mod = mods[l]
        sh1, sc1, gt1, sh2, sc2, gt2 = [m[:, None, :] for m in jnp.split(mod, 6, axis=-1)]
        gn = g_norm[l][:, None, :]
        proj = _inproj(x, gn[0], sc1, sh1, wts["w_in"][l], seq, tm_big)
        qh, ka, kb, vt, nrm = _rope(proj, cos_t, sin_t, wts["g_q"][l], wts["g_k"][l], bd, seq, tm)
        att = _attention(qh, ka, kb, vt, nrm, batch, seq, tm, tm)
        o_f = _hgrn(proj, lb_logits[0], batch, seq, tm, False, l)
        o_b = _hgrn(proj, lb_logits[1], batch, seq, tm, True, l)
        x = _mixout(att, o_f, o_b, proj, x, gt1, g_hgrn[l][None, :], gn[1],
                    wts["w_branch"][l, 0], wts["w_branch"][l, 1], wts["w_o"][l], seq, tm)
        if l % 2 == 0:
            x = _ffn(x, gn[2], sc2, sh2, gt2, gn[3], wts["w_ffn_in"][l // 2],
                     wts["w_ffn_out"][l // 2], seq, tm, FFN_CHUNK_DENSE)
        else:
            h, sel = _router(x, gn[2], sc2, sh2, wts["w_r"][l // 2], wts["b_r"][l // 2], seq, tm)
            moe_tile = min(MOE_TILE, 2 * batch * seq)
            src, texp, n_valid, pos_tiles = _route_plan(sel, moe_tile, tm)
            y_sorted = _moe_experts(h, src, texp, n_valid, wts["w_moe_in"][l // 2],
                                    wts["w_moe_out"][l // 2], moe_tile, FFN_CHUNK_EXPERT)
            x = _combine(pos_tiles, y_sorted, x, sel, gt2, gn[3], seq, tm)
    return x.reshape(batch, seq, D_MODEL)


def kernel(x_prompt, x_sample, c_prompt, c_sample, w_ada, b_ada, g_norm, w_in, g_qk, lb_logits,
           g_hgrn, w_branch, w_o, w_ffn_in, w_ffn_out, w_router, b_router, w_moe_in, w_moe_out):
    wts = _prep_weights(w_in, g_qk, w_branch, w_o, w_ffn_in, w_ffn_out, w_router, b_router,
                        w_moe_in, w_moe_out)
    n_p, n_s = c_prompt.shape[0], c_sample.shape[0]
    c_all = jnp.concatenate([c_prompt, c_sample], axis=0)
    pad = (-c_all.shape[0]) % SUBLANES
    c_all = jnp.pad(c_all, ((0, pad), (0, 0)))
    mods = _ada(c_all, w_ada, b_ada)
    y_p = _trunk(x_prompt, mods[:, :n_p], g_norm, lb_logits, g_hgrn, wts)
    y_s = _trunk(x_sample, mods[:, n_p:n_p + n_s], g_norm, lb_logits, g_hgrn, wts)
    return (y_p, y_s)
```

```python
import functools

import jax
import jax.numpy as jnp
import numpy as np
from jax import lax
from jax.experimental import pallas as pl
from jax.experimental.pallas import tpu as pltpu

F32 = jnp.float32
BF16 = jnp.bfloat16

D_MODEL = 1024
DEPTH = 4
GRID_W = 64
N_HEADS = 8
N_KV_HEADS = 2
HEAD_DIM = 64
KV_GROUP = N_HEADS // N_KV_HEADS
ROPE_F = HEAD_DIM // 4
ROPE_THETA = 10000.0
HG_HEADS = 4
HG_DK = 128
HG_DV = 128
ATT_Q = N_HEADS * HEAD_DIM
ATT_KV = N_KV_HEADS * HEAD_DIM
HG_F = HG_HEADS * HG_DK
D_FF = 2752
N_EXPERTS = 8
D_FF_EXPERT = 3584
EPS = 1e-6

LANES = 128
SUBLANES = 8
VMEM_LIMIT_BYTES = 56 * 1024 * 1024

COL_GA, COL_GB = 0, 1024
COL_HQ, COL_HFF, COL_HFB, COL_HI, COL_HG = 2048, 2560, 3072, 3584, 4096
COL_AQ, COL_AK, COL_AV = 4608, 5120, 5248
IN_COLS = 5376
IN_TILE_N = 1792

D_FF_PAD = 2816
FFN_CHUNK_DENSE = 1408
FFN_CHUNK_EXPERT = 1792
HG_CHUNK = 128
VT_ROWS = HEAD_DIM + 16
NRM_K_ROW = ATT_Q // LANES
SCORE_BOUND = 50.0
ATT_KEY_TILE = 1024
MOE_TILE = 512
SEL_E1, SEL_E2, SEL_W1, SEL_W2 = 0, 1, 2, 3


def _params(sem):
    return pltpu.CompilerParams(dimension_semantics=sem, vmem_limit_bytes=VMEM_LIMIT_BYTES)


def _sigmoid(x):
    return 1.0 / (1.0 + jnp.exp(-x))


def _silu(x):
    return x * _sigmoid(x)


def _rms(x, g):
    ms = jnp.mean(x * x, axis=-1, keepdims=True)
    return x * lax.rsqrt(ms + EPS) * g


def _ada_kernel(c_ref, w_ref, b_ref, o_ref):
    c = c_ref[...]
    o_ref[0] = jnp.dot(_silu(c), w_ref[0], preferred_element_type=F32,
                       precision=lax.Precision.HIGHEST) + b_ref[0]


def _ada(c, w_ada, b_ada):
    bp = c.shape[0]
    n_col = 6 * D_MODEL // D_MODEL
    return pl.pallas_call(
        _ada_kernel,
        grid=(DEPTH, n_col),
        in_specs=[
            pl.BlockSpec((bp, D_MODEL), lambda l, j: (0, 0)),
            pl.BlockSpec((1, D_MODEL, D_MODEL), lambda l, j: (l, 0, j)),
            pl.BlockSpec((1, 1, D_MODEL), lambda l, j: (l, 0, j)),
        ],
        out_specs=pl.BlockSpec((1, bp, D_MODEL), lambda l, j: (l, 0, j)),
        out_shape=jax.ShapeDtypeStruct((DEPTH, bp, 6 * D_MODEL), F32),
        compiler_params=_params(("arbitrary", "arbitrary")),
        name="ada_mod",
    )(c, w_ada, b_ada.reshape(DEPTH, 1, 6 * D_MODEL))


def _inproj_kernel(x_ref, g_ref, sc_ref, sh_ref, w_ref, o_ref, h_scr):
    @pl.when(pl.program_id(1) == 0)
    def _():
        y = _rms(x_ref[...], g_ref[...])
        h_scr[...] = (y * (1.0 + sc_ref[0]) + sh_ref[0]).astype(BF16)

    o_ref[...] = jnp.dot(h_scr[...], w_ref[...], preferred_element_type=F32).astype(BF16)


def _inproj(x, g, sc, sh, w, seq, tm):
    n = x.shape[0]
    per_b = seq // tm
    return pl.pallas_call(
        _inproj_kernel,
        grid=(n // tm, IN_COLS // IN_TILE_N),
        in_specs=[
            pl.BlockSpec((tm, D_MODEL), lambda i, j: (i, 0)),
            pl.BlockSpec((1, D_MODEL), lambda i, j: (0, 0)),
            pl.BlockSpec((1, 1, D_MODEL), lambda i, j: (i // per_b, 0, 0)),
            pl.BlockSpec((1, 1, D_MODEL), lambda i, j: (i // per_b, 0, 0)),
            pl.BlockSpec((D_MODEL, IN_TILE_N), lambda i, j: (0, j)),
        ],
        out_specs=pl.BlockSpec((tm, IN_TILE_N), lambda i, j: (i, j)),
        out_shape=jax.ShapeDtypeStruct((n, IN_COLS), BF16),
        scratch_shapes=[pltpu.VMEM((tm, D_MODEL), BF16)],
        compiler_params=_params(("parallel", "arbitrary")),
        name="in_proj",
    )(x, g, sc, sh, w)


def _rope_kernel(aq_ref, ak_ref, av_ref, cos_ref, sin_ref, gq_ref, gk_ref, bd_ref,
                 qh_ref, ka_ref, kb_ref, vt_ref, nrm_ref):
    tm = aq_ref.shape[0]
    lane = lax.broadcasted_iota(jnp.int32, (tm, LANES), 1)
    first_half = (lane % (2 * ROPE_F)) < ROPE_F
    low = lane < HEAD_DIM
    cos = cos_ref[...]
    sin = sin_ref[...]
    bd = bd_ref[...]

    def norm_rope(x, g):
        xx = x * x
        hi = xx.astype(BF16)
        lo = (xx - hi.astype(F32)).astype(BF16)
        ms = (jnp.dot(hi, bd, preferred_element_type=F32)
              + jnp.dot(lo, bd, preferred_element_type=F32))
        y = x * lax.rsqrt(ms + EPS) * g
        nxt = pltpu.roll(y, LANES - ROPE_F, 1)
        prv = pltpu.roll(y, ROPE_F, 1)
        return y * cos + jnp.where(first_half, nxt, prv) * sin

    def max_sq_norm(y):
        ss = jnp.dot((y * y).astype(BF16), bd, preferred_element_type=F32) * HEAD_DIM
        return jnp.max(ss, axis=0, keepdims=True)

    gq = gq_ref[...]
    scale = HEAD_DIM ** -0.5 * float(np.log2(np.e))
    nrm_ref[...] = jnp.zeros(nrm_ref.shape, F32)
    for j in range(ATT_Q // LANES):
        r = norm_rope(aq_ref[:, j * LANES:(j + 1) * LANES].astype(F32), gq) * scale
        qh_ref[:, (2 * j) * LANES:(2 * j + 1) * LANES] = jnp.where(low, r, 0.0).astype(BF16)
        qh_ref[:, (2 * j + 1) * LANES:(2 * j + 2) * LANES] = jnp.where(low, 0.0, r).astype(BF16)
        nrm_ref[0, j:j + 1, :] = max_sq_norm(r)

    k = norm_rope(ak_ref[...].astype(F32), gk_ref[...])
    nrm_ref[0, NRM_K_ROW:NRM_K_ROW + 1, :] = max_sq_norm(k)
    ka_ref[...] = k.astype(BF16)
    kb_ref[...] = pltpu.roll(k, HEAD_DIM, 1).astype(BF16)
    vt = av_ref[...].astype(F32).T.astype(BF16)
    ones = jnp.ones((VT_ROWS - HEAD_DIM, tm), BF16)
    for g in range(N_KV_HEADS):
        vt_ref[g, 0:HEAD_DIM, :] = vt[g * HEAD_DIM:(g + 1) * HEAD_DIM, :]
        vt_ref[g, HEAD_DIM:VT_ROWS, :] = ones


def _rope(proj, cos_t, sin_t, gq, gk, bd, seq, tm):
    n = proj.shape[0]
    per_b = seq // tm
    kv_spec = pl.BlockSpec((tm, LANES), lambda i: (i, 0))
    vt_spec = pl.BlockSpec((N_KV_HEADS, VT_ROWS, tm), lambda i: (0, 0, i))
    return pl.pallas_call(
        _rope_kernel,
        grid=(n // tm,),
        in_specs=[
            pl.BlockSpec((tm, ATT_Q), lambda i: (i, COL_AQ // ATT_Q)),
            pl.BlockSpec((tm, LANES), lambda i: (i, COL_AK // LANES)),
            pl.BlockSpec((tm, LANES), lambda i: (i, COL_AV // LANES)),
            pl.BlockSpec((tm, LANES), lambda i: (i % per_b, 0)),
            pl.BlockSpec((tm, LANES), lambda i: (i % per_b, 0)),
            pl.BlockSpec((1, LANES), lambda i: (0, 0)),
            pl.BlockSpec((1, LANES), lambda i: (0, 0)),
            pl.BlockSpec((LANES, LANES), lambda i: (0, 0)),
        ],
        out_specs=[pl.BlockSpec((tm, N_HEADS * LANES), lambda i: (i, 0)),
                   kv_spec, kv_spec, vt_spec,
                   pl.BlockSpec((1, SUBLANES, LANES), lambda i: (i, 0, 0))],
        out_shape=[jax.ShapeDtypeStruct((n, N_HEADS * LANES), BF16)]
        + [jax.ShapeDtypeStruct((n, LANES), BF16)] * 2
        + [jax.ShapeDtypeStruct((N_KV_HEADS, VT_ROWS, n), BF16),
           jax.ShapeDtypeStruct((n // tm, SUBLANES, LANES), F32)],
        compiler_params=_params(("parallel",)),
        name="qk_norm_rope",
    )(proj, proj, proj, cos_t, sin_t, gq, gk, bd)


def _attn_kernel(bounded_ref, q_ref, ka_ref, kb_ref, vt_ref, o_ref, m_scr, acc_scr):
    j = pl.program_id(2)
    step = (pl.program_id(0) * pl.num_programs(1) + pl.program_id(1)) * pl.num_programs(2) + j

    @pl.when(j == 0)
    def _():
        m_scr[...] = jnp.full(m_scr.shape, -jnp.inf, F32)
        acc_scr[...] = jnp.zeros(acc_scr.shape, F32)

    def scores(h):
        g, half = h // KV_GROUP, h % 2
        k = (ka_ref if g == half else kb_ref)[...]
        q = q_ref[:, h * LANES:(h + 1) * LANES]
        return lax.dot_general(k, q, (((1,), (1,)), ((), ())), preferred_element_type=F32)

    def all_heads(tile_max):
        st_next = scores(0)
        for h in range(N_HEADS):
            vt = vt_ref[h // KV_GROUP]
            st = st_next
            if h + 1 < N_HEADS:
                st_next = scores(h + 1)
            m_prev = m_scr[h]
            m_new = jnp.maximum(m_prev, tile_max(st))
            alpha = jnp.exp2(m_prev - m_new)
            pt = jnp.exp2((st - m_new).astype(BF16))
            acc_scr[h] = alpha * acc_scr[h] + jnp.dot(vt, pt, preferred_element_type=F32)
            m_scr[h] = m_new

    @pl.when(bounded_ref[step] != 0)
    def _():
        all_heads(lambda st: 0.0)

    @pl.when(bounded_ref[step] == 0)
    def _():
        all_heads(lambda st: jnp.max(st, axis=0, keepdims=True))

    @pl.when(j == pl.num_programs(2) - 1)
    def _():
        for jj in range(N_HEADS // 2):
            pair = []
            for h in (2 * jj, 2 * jj + 1):
                acc = acc_scr[h]
                pair.append(acc[0:HEAD_DIM, :] / acc[HEAD_DIM:HEAD_DIM + 1, :])
            o_ref[:, jj * LANES:(jj + 1) * LANES] = jnp.concatenate(pair, axis=0).T.astype(BF16)


def _attention(qh, ka, kb, vt, nrm, batch, seq, tq, tk):
    n = qh.shape[0]
    nq, nk = seq // tq, seq // tk
    halves = jnp.array([0, HEAD_DIM])
    qn2 = nrm[:, :NRM_K_ROW, :][:, :, halves].reshape(batch, nq, -1, N_HEADS).max(axis=2)
    qn2 = qn2.reshape(batch, nq, 1, N_HEADS)
    kn2 = jnp.repeat(nrm[:, NRM_K_ROW, :][:, halves], KV_GROUP, axis=1)
    kn2 = kn2.reshape(batch, nk, -1, N_HEADS).max(axis=2).reshape(batch, 1, nk, N_HEADS)
    bounded = jnp.all(qn2 * kn2 <= SCORE_BOUND ** 2, axis=-1).astype(jnp.int32).reshape(-1)
    k_spec = pl.BlockSpec((tk, LANES), lambda b, i, j, fl: (b * nk + j, 0))
    vt_spec = pl.BlockSpec((N_KV_HEADS, VT_ROWS, tk), lambda b, i, j, fl: (0, 0, b * nk + j))
    grid_spec = pltpu.PrefetchScalarGridSpec(
        num_scalar_prefetch=1,
        grid=(batch, nq, nk),
        in_specs=[pl.BlockSpec((tq, N_HEADS * LANES), lambda b, i, j, fl: (b * nq + i, 0)),
                  k_spec, k_spec, vt_spec],
        out_specs=pl.BlockSpec((tq, ATT_Q), lambda b, i, j, fl: (b * nq + i, 0)),
        scratch_shapes=[pltpu.VMEM((N_HEADS, 1, tq), F32),
                        pltpu.VMEM((N_HEADS, VT_ROWS, tq), F32)],
    )
    return pl.pallas_call(
        _attn_kernel,
        grid_spec=grid_spec,
        out_shape=jax.ShapeDtypeStruct((n, ATT_Q), BF16),
        compiler_params=_params(("parallel", "parallel", "arbitrary")),
        name="gqa_attention",
    )(bounded, qh, ka, kb, vt)


def _level_ref(b, m, rev):
    c = b.shape[0]
    g = 2 * m
    if g >= SUBLANES:
        b3 = b.reshape(c // g, g, LANES)
        r = m if rev else m - 1
        return jnp.broadcast_to(b3[:, r:r + 1, :], b3.shape).reshape(c, LANES)
    b3 = b.reshape(c // SUBLANES, SUBLANES, LANES)
    sub = lax.broadcasted_iota(jnp.int32, b3.shape, 1)
    out = None
    for p0 in range(0, SUBLANES, g):
        r = p0 + (m if rev else m - 1)
        cand = jnp.broadcast_to(b3[:, r:r + 1, :], b3.shape)
        out = cand if out is None else jnp.where(sub >= p0, cand, out)
    return out.reshape(c, LANES)


def _chunk_cumsum(x, tri):
    hi = x.astype(BF16)
    r1 = x - hi.astype(F32)
    mid = r1.astype(BF16)
    lo = (r1 - mid.astype(F32)).astype(BF16)
    return (jnp.dot(tri, hi, preferred_element_type=F32)
            + jnp.dot(tri, mid, preferred_element_type=F32)
            + jnp.dot(tri, lo, preferred_element_type=F32))


def _neg_abs(z):
    bits = lax.bitcast_convert_type(z, jnp.int32) | jnp.int32(-2 ** 31)
    return lax.bitcast_convert_type(bits, F32)


def _hgrn_kernel(hq_ref, hf_ref, hi_ref, lbl_ref, o_ref, st_scr, *, rev, layer, chunk):
    c = chunk
    n_chunks = hq_ref.shape[0] // c

    @pl.when(pl.program_id(1) == 0)
    def _():
        st_scr[...] = jnp.zeros(st_scr.shape, F32)

    lbl = lbl_ref[...]
    lbe = jnp.exp(lbl - jnp.max(lbl, axis=0, keepdims=True))
    lbp = lbe / jnp.sum(lbe, axis=0, keepdims=True)
    lb_all = jnp.zeros((1, HG_F), F32)
    for l in range(1, layer + 1):
        lb_all = lb_all + lbp[l:l + 1, :]

    row = lax.broadcasted_iota(jnp.int32, (c, LANES), 0)
    row_a = lax.broadcasted_iota(jnp.int32, (c, c), 0)
    col_a = lax.broadcasted_iota(jnp.int32, (c, c), 1)
    tri = jnp.where((row_a <= col_a) if rev else (row_a >= col_a), 1.0, 0.0).astype(BF16)
    lev = 31 - lax.clz(row_a ^ col_a)
    lev = jnp.where((row_a < col_a) if rev else (row_a > col_a), lev, -1)
    q_scale = HG_DK ** -0.5
    log2e = float(np.log2(np.e))
    nt_dims = (((1,), (1,)), ((), ()))

    def body(ci, carry):
        cc = (n_chunks - 1 - ci) if rev else ci
        r0 = pl.multiple_of(cc * c, c)
        heads = range(HG_HEADS)
        cols = [slice(h * LANES, (h + 1) * LANES) for h in heads]
        v = [hi_ref[pl.ds(r0, c), cols[h]] for h in heads]
        q = [_silu(hq_ref[pl.ds(r0, c), cols[h]].astype(F32)) * q_scale for h in heads]
        f = [lb_all[:, cols[h]] + (1.0 - lb_all[:, cols[h]])
             * _sigmoid(hf_ref[pl.ds(r0, c), cols[h]].astype(F32)) for h in heads]
        k = [1.0 - f[h] for h in heads]
        b = [_chunk_cumsum(jnp.log(f[h]) * log2e, tri) for h in heads]
        a = [jnp.zeros((c, c), F32) for h in heads]
        m = c // 2
        while m >= 1:
            upper = (row & m) != 0
            for h in heads:
                e = jnp.exp2(_neg_abs(b[h] - _level_ref(b[h], m, rev)).astype(BF16))
                w = jnp.where(upper, k[h], q[h]) if rev else jnp.where(upper, q[h], k[h])
                w = w.astype(BF16) * e
                al = lax.dot_general(w, w, nt_dims, preferred_element_type=F32)
                a[h] = jnp.where(lev == (m.bit_length() - 1), al, a[h])
            m //= 2
        for h in heads:
            st = st_scr[h]
            o = lax.dot_general((q[h] * jnp.exp2(b[h])).astype(BF16), st.astype(BF16), nt_dims,
                                preferred_element_type=F32)
            diag = jnp.sum(q[h] * k[h], axis=-1, keepdims=True)
            o = (o + jnp.dot(a[h].astype(BF16), v[h], preferred_element_type=F32)
                 + diag * v[h].astype(F32))
            o_ref[pl.ds(r0, c), cols[h]] = o.astype(o_ref.dtype)
            b_end = b[h][0:1, :] if rev else b[h][c - 1:c, :]
            k_end = (k[h] * jnp.exp2(b_end - b[h])).astype(BF16)
            upd = lax.dot_general(v[h], k_end, (((0,), (0,)), ((), ())), preferred_element_type=F32)
            st_scr[h] = st * jnp.exp2(b_end) + upd
        return carry

    lax.fori_loop(0, n_chunks, body, 0, unroll=True)


def _hgrn(proj, lb_logits_dir, batch, seq, tt, rev, layer):
    n = proj.shape[0]
    nt = seq // tt
    col_f = (COL_HFB if rev else COL_HFF) // HG_F

    def rows(b, i):
        return b * nt + ((nt - 1 - i) if rev else i)

    return pl.pallas_call(
        functools.partial(_hgrn_kernel, rev=rev, layer=layer, chunk=min(HG_CHUNK, tt)),
        grid=(batch, nt),
        in_specs=[
            pl.BlockSpec((tt, HG_F), lambda b, i: (rows(b, i), COL_HQ // HG_F)),
            pl.BlockSpec((tt, HG_F), lambda b, i: (rows(b, i), col_f)),
            pl.BlockSpec((tt, HG_F), lambda b, i: (rows(b, i), COL_HI // HG_F)),
            pl.BlockSpec((DEPTH, HG_F), lambda b, i: (0, 0)),
        ],
        out_specs=pl.BlockSpec((tt, HG_F), lambda b, i: (rows(b, i), 0)),
        out_shape=jax.ShapeDtypeStruct((n, HG_F), BF16),
        scratch_shapes=[pltpu.VMEM((HG_HEADS, HG_DV, HG_DK), F32)],
        compiler_params=_params(("parallel", "arbitrary")),
        name="hgrn2_bwd" if rev else "hgrn2_fwd",
    )(proj, proj, proj, lb_logits_dir)


def _mixout_kernel(att_ref, of_ref, ob_ref, hg_ref, ga_ref, gb_ref, x_ref, gt_ref, ghg_ref,
                   gn_ref, wa_ref, wb_ref, wo_ref, o_ref):
    ghg = ghg_ref[...]
    recs = []
    for h in range(HG_HEADS):
        cols = slice(h * LANES, (h + 1) * LANES)
        o = of_ref[:, cols].astype(F32) + ob_ref[:, cols].astype(F32)
        recs.append((_rms(o, ghg) * _silu(hg_ref[:, cols].astype(F32))).astype(BF16))
    rec = jnp.concatenate(recs, axis=1)
    pa = jnp.dot(att_ref[...], wa_ref[...], preferred_element_type=F32)
    pb = jnp.dot(rec, wb_ref[...], preferred_element_type=F32)
    merged = (_sigmoid(ga_ref[...].astype(F32)) * pa + _sigmoid(gb_ref[...].astype(F32)) * pb)
    y = jnp.dot(merged.astype(BF16), wo_ref[...], preferred_element_type=F32)
    o_ref[...] = x_ref[...] + gt_ref[0] * _rms(y, gn_ref[...])


def _mixout(att, o_f, o_b, proj, x, gt, g_hgrn, g_norm, w_a, w_b, w_o, seq, tm):
    n = x.shape[0]
    per_b = seq // tm
    half = pl.BlockSpec((tm, ATT_Q), lambda i: (i, 0))
    const = lambda shape: pl.BlockSpec(shape, lambda i: (0,) * len(shape))
    return pl.pallas_call(
        _mixout_kernel,
        grid=(n // tm,),
        in_specs=[
            half, half, half,
            pl.BlockSpec((tm, HG_F), lambda i: (i, COL_HG // HG_F)),
            pl.BlockSpec((tm, D_MODEL), lambda i: (i, COL_GA // D_MODEL)),
            pl.BlockSpec((tm, D_MODEL), lambda i: (i, COL_GB // D_MODEL)),
            pl.BlockSpec((tm, D_MODEL), lambda i: (i, 0)),
            pl.BlockSpec((1, 1, D_MODEL), lambda i: (i // per_b, 0, 0)),
            const((1, LANES)), const((1, D_MODEL)),
            const((ATT_Q, D_MODEL)), const((HG_F, D_MODEL)), const((D_MODEL, D_MODEL)),
        ],
        out_specs=pl.BlockSpec((tm, D_MODEL), lambda i: (i, 0)),
        out_shape=jax.ShapeDtypeStruct((n, D_MODEL), F32),
        compiler_params=_params(("parallel",)),
        name="mixer_out",
    )(att, o_f, o_b, proj, proj, proj, x, gt, g_hgrn, g_norm, w_a, w_b, w_o)


def _router_kernel(x_ref, g_ref, sc_ref, sh_ref, wr_ref, br_ref, h_ref, sel_ref):
    h = _rms(x_ref[...], g_ref[...]) * (1.0 + sc_ref[0]) + sh_ref[0]
    tm = h.shape[0]
    for j in range(D_MODEL // LANES):
        h_ref[pl.ds(j, tm, stride=SUBLANES), :] = h[:, j * LANES:(j + 1) * LANES]
    logits = jnp.dot(h, wr_ref[...], preferred_element_type=F32,
                     precision=lax.Precision.HIGHEST) + br_ref[...]
    lane = lax.broadcasted_iota(jnp.int32, logits.shape, 1).astype(F32)
    logits = jnp.where(lane < N_EXPERTS, logits, -jnp.inf)
    m1 = jnp.max(logits, axis=-1, keepdims=True)
    i1 = jnp.min(jnp.where(logits == m1, lane, float(LANES)), axis=-1, keepdims=True)
    rest = jnp.where(lane == i1, -jnp.inf, logits)
    m2 = jnp.max(rest, axis=-1, keepdims=True)
    i2 = jnp.min(jnp.where(rest == m2, lane, float(LANES)), axis=-1, keepdims=True)
    e2 = jnp.exp(m2 - m1)
    w1 = 1.0 / (1.0 + e2)
    w2 = e2 / (1.0 + e2)
    sel_ref[...] = (jnp.where(lane == SEL_E1, i1, 0.0) + jnp.where(lane == SEL_E2, i2, 0.0)
                    + jnp.where(lane == SEL_W1, w1, 0.0) + jnp.where(lane == SEL_W2, w2, 0.0))


def _router(x, g, sc, sh, w_r, b_r, seq, tm):
    n = x.shape[0]
    per_b = seq // tm
    return pl.pallas_call(
        _router_kernel,
        grid=(n // tm,),
        in_specs=[
            pl.BlockSpec((tm, D_MODEL), lambda i: (i, 0)),
            pl.BlockSpec((1, D_MODEL), lambda i: (0, 0)),
            pl.BlockSpec((1, 1, D_MODEL), lambda i: (i // per_b, 0, 0)),
            pl.BlockSpec((1, 1, D_MODEL), lambda i: (i // per_b, 0, 0)),
            pl.BlockSpec((D_MODEL, LANES), lambda i: (0, 0)),
            pl.BlockSpec((1, LANES), lambda i: (0, 0)),
        ],
        out_specs=[pl.BlockSpec((tm * SUBLANES, LANES), lambda i: (i, 0)),
                   pl.BlockSpec((tm, LANES), lambda i: (i, 0))],
        out_shape=[jax.ShapeDtypeStruct((n * SUBLANES, LANES), F32),
                   jax.ShapeDtypeStruct((n, LANES), F32)],
        compiler_params=_params(("parallel",)),
        name="router_top2",
    )(x, g, sc, sh, w_r, b_r)


def _route_plan(sel, tm, tmc):
    n = sel.shape[0]
    r = 2 * n
    flat_e = sel[:, :2].astype(jnp.int32).reshape(-1)
    onehot = (flat_e[:, None] == jnp.arange(N_EXPERTS, dtype=jnp.int32)[None, :]).astype(jnp.int32)
    csum = jnp.cumsum(onehot, axis=0)
    rank = jnp.sum(csum * onehot, axis=1) - 1
    counts = csum[-1]
    padded = (counts + tm - 1) // tm * tm
    ends = jnp.cumsum(padded)
    pos = (ends - padded)[flat_e] + rank
    n_tiles = r // tm + N_EXPERTS
    n_valid = (ends[-1] // tm).astype(jnp.int32)
    tile_start = jnp.arange(n_tiles, dtype=jnp.int32) * tm
    texp = jnp.minimum(jnp.searchsorted(ends, tile_start, side="right"), N_EXPERTS - 1)
    texp = jnp.where(tile_start < ends[-1], texp, texp[jnp.maximum(n_valid - 1, 0)])
    order = jnp.argsort(flat_e, stable=True).astype(jnp.int32)
    row_e = jnp.repeat(texp, tm)
    local = jnp.arange(n_tiles * tm, dtype=jnp.int32) - (ends - padded)[row_e]
    sorted_idx = jnp.clip((jnp.cumsum(counts) - counts)[row_e] + local, 0, r - 1)
    src = jnp.where(local < counts[row_e], order[sorted_idx] // 2, 0)
    src = src * SUBLANES
    pos = pos * SUBLANES
    pos_tiles = pos.reshape(n // tmc, tmc, 2).transpose(0, 2, 1).reshape(n // tmc, 2 * tmc)
    return src.reshape(n_tiles, tm), texp.astype(jnp.int32), n_valid.reshape(1), pos_tiles


def _moe_kernel(texp_ref, nv_ref, src_hbm, h_hbm, wa_ref, wb_ref, wo_ref, y_ref,
                hbuf, hb16, acc, idx_smem, gsem, isem, *, tm):
    t = pl.program_id(0)
    f = pl.program_id(1)
    nv = nv_ref[0]
    slot = t % 2
    other = 1 - slot
    n_lane_blocks = D_MODEL // LANES

    def idx_copy(tile, s):
        return pltpu.make_async_copy(src_hbm.at[pl.ds(tile, 1)], idx_smem.at[pl.ds(s, 1)], isem.at[s])

    def issue_row(s, r):
        src_row = pl.multiple_of(idx_smem[s, r], SUBLANES)
        dst_row = pl.multiple_of(r * SUBLANES, SUBLANES)
        pltpu.make_async_copy(h_hbm.at[pl.ds(src_row, SUBLANES)],
                              hbuf.at[s, pl.ds(dst_row, SUBLANES)], gsem.at[s]).start()

    def wait_rows(s):
        pltpu.make_async_copy(h_hbm.at[pl.ds(0, tm * SUBLANES)], hbuf.at[s], gsem.at[s]).wait()

    def swiglu_chunk():
        h = hb16[...]
        a = jnp.dot(h, wa_ref[0], preferred_element_type=F32)
        b = jnp.dot(h, wb_ref[0], preferred_element_type=F32)
        return jnp.dot((_silu(a) * b).astype(BF16), wo_ref[0], preferred_element_type=F32)

    @pl.when((t == 0) & (f == 0))
    def _():
        first = idx_copy(0, 0)
        first.start()
        first.wait()
        lax.fori_loop(0, tm, lambda r, c: (issue_row(0, r), c)[1], 0, unroll=8)

    @pl.when((f == 0) & (t <= nv))
    def _():
        wait_rows(slot)
        for j in range(n_lane_blocks):
            hb16[:, j * LANES:(j + 1) * LANES] = (
                hbuf[slot, pl.ds(j, tm, stride=SUBLANES), :].astype(BF16))

    @pl.when((f == 0) & (t < nv))
    def _():
        idx_copy(t + 1, other).start()
        acc[...] = swiglu_chunk()

    @pl.when((f == 1) & (t < nv))
    def _():
        idx_copy(t + 1, other).wait()
        for r in range(tm):
            issue_row(other, r)
        y = swiglu_chunk()
        for j in range(n_lane_blocks):
            cols = slice(j * LANES, (j + 1) * LANES)
            y_ref[pl.ds(j, tm, stride=SUBLANES), :] = acc[:, cols] + y[:, cols]

    @pl.when((t >= nv) & (f == 0))
    def _():
        y_ref[...] = jnp.zeros(y_ref.shape, F32)


def _moe_experts(h, src, texp, n_valid, w_in, w_out, tm, chunk):
    n_tiles = src.shape[0]
    ff = w_in.shape[2] // 2
    n_fc = ff // chunk
    assert n_fc == 2, "the kernel's gather schedule and accumulation are written for two chunks"

    def fidx(t, f, nv):
        return jnp.where(t < nv[0], f, n_fc - 1)

    grid_spec = pltpu.PrefetchScalarGridSpec(
        num_scalar_prefetch=2,
        grid=(n_tiles, n_fc),
        in_specs=[
            pl.BlockSpec(memory_space=pl.ANY),
            pl.BlockSpec(memory_space=pl.ANY),
            pl.BlockSpec((1, D_MODEL, chunk), lambda t, f, te, nv: (te[t], 0, fidx(t, f, nv))),
            pl.BlockSpec((1, D_MODEL, chunk), lambda t, f, te, nv: (te[t], 0, n_fc + fidx(t, f, nv))),
            pl.BlockSpec((1, chunk, D_MODEL), lambda t, f, te, nv: (te[t], fidx(t, f, nv), 0)),
        ],
        out_specs=pl.BlockSpec((tm * SUBLANES, LANES), lambda t, f, te, nv: (t, 0)),
        scratch_shapes=[
            pltpu.VMEM((2, tm * SUBLANES, LANES), F32),
            pltpu.VMEM((tm, D_MODEL), BF16),
            pltpu.VMEM((tm, D_MODEL), F32),
            pltpu.SMEM((2, tm), jnp.int32),
            pltpu.SemaphoreType.DMA((2,)),
            pltpu.SemaphoreType.DMA((2,)),
        ],
    )
    return pl.pallas_call(
        functools.partial(_moe_kernel, tm=tm),
        grid_spec=grid_spec,
        out_shape=jax.ShapeDtypeStruct((n_tiles * tm * SUBLANES, LANES), F32),
        compiler_params=_params(("arbitrary", "arbitrary")),
        name="moe_experts",
    )(texp, n_valid, src, h, w_in, w_in, w_out)


def _combine_kernel(pos_hbm, y_hbm, x_ref, sel_ref, gt_ref, gn_ref, o_ref, ybuf, idx_smem, gsem, isem,
                    *, tmc, n_steps):
    i = pl.program_id(0)
    slot = i % 2
    other = 1 - slot
    rows = 2 * tmc

    def idx_copy(tile, s):
        return pltpu.make_async_copy(pos_hbm.at[pl.ds(tile, 1)], idx_smem.at[pl.ds(s, 1)], isem.at[s])

    def issue_rows(s):
        def body(r, carry):
            src_row = pl.multiple_of(idx_smem[s, r], SUBLANES)
            dst_row = pl.multiple_of(r * SUBLANES, SUBLANES)
            pltpu.make_async_copy(y_hbm.at[pl.ds(src_row, SUBLANES)],
                                  ybuf.at[s, pl.ds(dst_row, SUBLANES)], gsem.at[s]).start()
            return carry
        lax.fori_loop(0, rows, body, 0, unroll=8)

    @pl.when(i == 0)
    def _():
        first = idx_copy(0, 0)
        first.start()
        first.wait()
        issue_rows(0)
        if n_steps > 1:
            idx_copy(1, 1).start()

    @pl.when(i + 1 < n_steps)
    def _():
        idx_copy(i + 1, other).wait()
        issue_rows(other)

        @pl.when(i + 2 < n_steps)
        def _():
            idx_copy(i + 2, slot).start()

    pltpu.make_async_copy(y_hbm.at[pl.ds(0, rows * SUBLANES)], ybuf.at[slot], gsem.at[slot]).wait()

    sel = sel_ref[...]
    w1 = sel[:, SEL_W1:SEL_W1 + 1]
    w2 = sel[:, SEL_W2:SEL_W2 + 1]
    second = tmc * SUBLANES
    y = jnp.concatenate(
        [w1 * ybuf[slot, pl.ds(j, tmc, stride=SUBLANES), :]
         + w2 * ybuf[slot, pl.ds(second + j, tmc, stride=SUBLANES), :]
         for j in range(D_MODEL // LANES)], axis=1)
    o_ref[...] = x_ref[...] + gt_ref[0] * _rms(y, gn_ref[...])


def _combine(pos_tiles, y_sorted, x, sel, gt, gn, seq, tmc):
    n = x.shape[0]
    per_b = seq // tmc
    return pl.pallas_call(
        functools.partial(_combine_kernel, tmc=tmc, n_steps=n // tmc),
        grid=(n // tmc,),
        in_specs=[
            pl.BlockSpec(memory_space=pl.ANY),
            pl.BlockSpec(memory_space=pl.ANY),
            pl.BlockSpec((tmc, D_MODEL), lambda i: (i, 0)),
            pl.BlockSpec((tmc, LANES), lambda i: (i, 0)),
            pl.BlockSpec((1, 1, D_MODEL), lambda i: (i // per_b, 0, 0)),
            pl.BlockSpec((1, D_MODEL), lambda i: (0, 0)),
        ],
        out_specs=pl.BlockSpec((tmc, D_MODEL), lambda i: (i, 0)),
        out_shape=jax.ShapeDtypeStruct((n, D_MODEL), F32),
        scratch_shapes=[
            pltpu.VMEM((2, 2 * tmc * SUBLANES, LANES), F32),
            pltpu.SMEM((2, 2 * tmc), jnp.int32),
            pltpu.SemaphoreType.DMA((2,)),
            pltpu.SemaphoreType.DMA((2,)),
        ],
        compiler_params=_params(("arbitrary",)),
        name="moe_combine",
    )(pos_tiles, y_sorted, x, sel, gt, gn)


def _ffn_kernel(x_ref, g_ref, sc_ref, sh_ref, gt_ref, gn_ref, wa_ref, wb_ref, wo_ref,
                o_ref, h_scr, acc_scr):
    f = pl.program_id(1)

    @pl.when(f == 0)
    def _():
        y = _rms(x_ref[...], g_ref[...])
        h_scr[...] = (y * (1.0 + sc_ref[0]) + sh_ref[0]).astype(BF16)
        acc_scr[...] = jnp.zeros(acc_scr.shape, F32)

    h = h_scr[...]
    a = jnp.dot(h, wa_ref[...], preferred_element_type=F32)
    b = jnp.dot(h, wb_ref[...], preferred_element_type=F32)
    acc_scr[...] += jnp.dot((_silu(a) * b).astype(BF16), wo_ref[...], preferred_element_type=F32)

    @pl.when(f == pl.num_programs(1) - 1)
    def _():
        o_ref[...] = x_ref[...] + gt_ref[0] * _rms(acc_scr[...], gn_ref[...])


def _ffn(x, g, sc, sh, gt, gn, w_in, w_out, seq, tm, chunk):
    n = x.shape[0]
    per_b = seq // tm
    n_fc = w_in.shape[1] // 2 // chunk
    vec = pl.BlockSpec((1, D_MODEL), lambda i, f: (0, 0))
    mod = pl.BlockSpec((1, 1, D_MODEL), lambda i, f: (i // per_b, 0, 0))
    return pl.pallas_call(
        _ffn_kernel,
        grid=(n // tm, n_fc),
        in_specs=[
            pl.BlockSpec((tm, D_MODEL), lambda i, f: (i, 0)), vec, mod, mod, mod, vec,
            pl.BlockSpec((D_MODEL, chunk), lambda i, f: (0, f)),
            pl.BlockSpec((D_MODEL, chunk), lambda i, f: (0, n_fc + f)),
            pl.BlockSpec((chunk, D_MODEL), lambda i, f: (f, 0)),
        ],
        out_specs=pl.BlockSpec((tm, D_MODEL), lambda i, f: (i, 0)),
        out_shape=jax.ShapeDtypeStruct((n, D_MODEL), F32),
        scratch_shapes=[pltpu.VMEM((tm, D_MODEL), BF16), pltpu.VMEM((tm, D_MODEL), F32)],
        compiler_params=_params(("parallel", "arbitrary")),
        name="dense_ffn",
    )(x, g, sc, sh, gt, gn, w_in, w_in, w_out)


def _rope_tables(seq):
    t = jnp.arange(seq, dtype=jnp.int32)
    row = (t // GRID_W).astype(F32)
    col = (t % GRID_W).astype(F32)
    inv = ROPE_THETA ** (-jnp.arange(ROPE_F, dtype=F32) / ROPE_F)
    ang_r = row[:, None] * inv
    ang_c = col[:, None] * inv
    cos = jnp.concatenate([jnp.cos(ang_r)] * 2 + [jnp.cos(ang_c)] * 2, axis=1)
    sin = jnp.concatenate([-jnp.sin(ang_r), jnp.sin(ang_r), -jnp.sin(ang_c), jnp.sin(ang_c)], axis=1)
    return jnp.tile(cos, (1, 2)), jnp.tile(sin, (1, 2))


def _prep_weights(w_in, g_qk, w_branch, w_o, w_ffn_in, w_ffn_out, w_router, b_router,
                  w_moe_in, w_moe_out):
    o_aq, o_ak, o_av, o_hq = 0, ATT_Q, ATT_Q + ATT_KV, ATT_Q + 2 * ATT_KV
    o_ga = o_hq + 5 * HG_F
    w_in_p = jnp.concatenate(
        [w_in[:, :, o_ga:], w_in[:, :, o_hq:o_ga], w_in[:, :, o_aq:o_hq]], axis=2).astype(BF16)
    pad_c = D_FF_PAD - D_FF
    w_fa = jnp.pad(w_ffn_in[:, :, :D_FF], ((0, 0), (0, 0), (0, pad_c)))
    w_fb = jnp.pad(w_ffn_in[:, :, D_FF:], ((0, 0), (0, 0), (0, pad_c)))
    w_ffn_in_p = jnp.concatenate([w_fa, w_fb], axis=2).astype(BF16)
    w_ffn_out_p = jnp.pad(w_ffn_out, ((0, 0), (0, pad_c), (0, 0))).astype(BF16)
    w_r_p = jnp.pad(w_router, ((0, 0), (0, 0), (0, LANES - N_EXPERTS)))
    b_r_p = jnp.pad(b_router, ((0, 0), (0, LANES - N_EXPERTS)))[:, None, :]
    g_q = jnp.tile(g_qk[:, 0, :], (1, LANES // HEAD_DIM))[:, None, :]
    g_k = jnp.tile(g_qk[:, 1, :], (1, LANES // HEAD_DIM))[:, None, :]
    return dict(w_in=w_in_p, w_branch=w_branch.astype(BF16), w_o=w_o.astype(BF16),
                w_ffn_in=w_ffn_in_p, w_ffn_out=w_ffn_out_p, w_r=w_r_p, b_r=b_r_p,
                w_moe_in=w_moe_in.astype(BF16), w_moe_out=w_moe_out.astype(BF16),
                g_q=g_q, g_k=g_k)


def _tile(seq, want):
    return min(seq, want)


def _trunk(x3, mods, g_norm, lb_logits, g_hgrn, wts):
    batch, seq, _ = x3.shape
    x = x3.reshape(batch * seq, D_MODEL)
    cos_t, sin_t = _rope_tables(seq)
    idx = np.arange(LANES)
    bd = jnp.asarray((idx[:, None] // HEAD_DIM == idx[None, :] // HEAD_DIM) / HEAD_DIM, BF16)
    tm_big = _tile(seq, 1024)
    tm = _tile(seq, 512)
    for l in range(DEPTH):
        mod = mods[l]
        sh1, sc1, gt1, sh2, sc2, gt2 = [m[:, None, :] for m in jnp.split(mod, 6, axis=-1)]
        gn = g_norm[l][:, None, :]
        proj = _inproj(x, gn[0], sc1, sh1, wts["w_in"][l], seq, tm_big)
        qh, ka, kb, vt, nrm = _rope(proj, cos_t, sin_t, wts["g_q"][l], wts["g_k"][l], bd, seq, tm)
        att = _attention(qh, ka, kb, vt, nrm, batch, seq, tm, min(seq, ATT_KEY_TILE))
        o_f = _hgrn(proj, lb_logits[0], batch, seq, tm, False, l)
        o_b = _hgrn(proj, lb_logits[1], batch, seq, tm, True, l)
        x = _mixout(att, o_f, o_b, proj, x, gt1, g_hgrn[l][None, :], gn[1],
                    wts["w_branch"][l, 0], wts["w_branch"][l, 1], wts["w_o"][l], seq, tm)
        if l % 2 == 0:
            x = _ffn(x, gn[2], sc2, sh2, gt2, gn[3], wts["w_ffn_in"][l // 2],
                     wts["w_ffn_out"][l // 2], seq, tm, FFN_CHUNK_DENSE)
        else:
            h, sel = _router(x, gn[2], sc2, sh2, wts["w_r"][l // 2], wts["b_r"][l // 2], seq, tm)
            moe_tile = min(MOE_TILE, 2 * batch * seq)
            src, texp, n_valid, pos_tiles = _route_plan(sel, moe_tile, tm)
            y_sorted = _moe_experts(h, src, texp, n_valid, wts["w_moe_in"][l // 2],
                                    wts["w_moe_out"][l // 2], moe_tile, FFN_CHUNK_EXPERT)
            x = _combine(pos_tiles, y_sorted, x, sel, gt2, gn[3], seq, tm)
    return x.reshape(batch, seq, D_MODEL)


def kernel(x_prompt, x_sample, c_prompt, c_sample, w_ada, b_ada, g_norm, w_in, g_qk, lb_logits,
           g_hgrn, w_branch, w_o, w_ffn_in, w_ffn_out, w_router, b_router, w_moe_in, w_moe_out):
    wts = _prep_weights(w_in, g_qk, w_branch, w_o, w_ffn_in, w_ffn_out, w_router, b_router,
                        w_moe_in, w_moe_out)
    n_p, n_s = c_prompt.shape[0], c_sample.shape[0]
    c_all = jnp.concatenate([c_prompt, c_sample], axis=0)
    pad = (-c_all.shape[0]) % SUBLANES
    c_all = jnp.pad(c_all, ((0, pad), (0, 0)))
    mods = _ada(c_all, w_ada, b_ada)
    y_p = _trunk(x_prompt, mods[:, :n_p], g_norm, lb_logits, g_hgrn, wts)
    y_s = _trunk(x_sample, mods[:, n_p:n_p + n_s], g_norm, lb_logits, g_hgrn, wts)
    return (y_p, y_s)
```

```python
import functools

import jax
import jax.numpy as jnp
import numpy as np
from jax import lax
from jax.experimental import pallas as pl
from jax.experimental.pallas import tpu as pltpu

F32 = jnp.float32
BF16 = jnp.bfloat16

D_MODEL = 1024
DEPTH = 4
GRID_W = 64
N_HEADS = 8
N_KV_HEADS = 2
HEAD_DIM = 64
KV_GROUP = N_HEADS // N_KV_HEADS
ROPE_F = HEAD_DIM // 4
ROPE_THETA = 10000.0
HG_HEADS = 4
HG_DK = 128
HG_DV = 128
ATT_Q = N_HEADS * HEAD_DIM
ATT_KV = N_KV_HEADS * HEAD_DIM
HG_F = HG_HEADS * HG_DK
D_FF = 2752
N_EXPERTS = 8
D_FF_EXPERT = 3584
EPS = 1e-6

LANES = 128
SUBLANES = 8
VMEM_LIMIT_BYTES = 56 * 1024 * 1024

COL_GA, COL_GB = 0, 1024
COL_HQ, COL_HFF, COL_HFB, COL_HI, COL_HG = 2048, 2560, 3072, 3584, 4096
COL_AQ, COL_AK, COL_AV = 4608, 5120, 5248
IN_COLS = 5376
IN_TILE_N = 1792

D_FF_PAD = 2816
FFN_CHUNK_DENSE = 1408
FFN_CHUNK_EXPERT = 1792
HG_CHUNK = 128
VT_ROWS = HEAD_DIM + 16
NRM_K_ROW = ATT_Q // LANES
SCORE_BOUND = 50.0
ATT_KEY_TILE = 1024
MOE_TILE = 512
SEL_E1, SEL_E2, SEL_W1, SEL_W2 = 0, 1, 2, 3


def _params(sem):
    return pltpu.CompilerParams(dimension_semantics=sem, vmem_limit_bytes=VMEM_LIMIT_BYTES)


def _sigmoid(x):
    return 1.0 / (1.0 + jnp.exp(-x))


def _silu(x):
    return x * _sigmoid(x)


def _rms(x, g):
    ms = jnp.mean(x * x, axis=-1, keepdims=True)
    return x * lax.rsqrt(ms + EPS) * g


def _ada_kernel(c_ref, w_ref, b_ref, o_ref):
    c = c_ref[...]
    o_ref[0] = jnp.dot(_silu(c), w_ref[0], preferred_element_type=F32,
                       precision=lax.Precision.HIGHEST) + b_ref[0]


def _ada(c, w_ada, b_ada):
    bp = c.shape[0]
    n_col = 6 * D_MODEL // D_MODEL
    return pl.pallas_call(
        _ada_kernel,
        grid=(DEPTH, n_col),
        in_specs=[
            pl.BlockSpec((bp, D_MODEL), lambda l, j: (0, 0)),
            pl.BlockSpec((1, D_MODEL, D_MODEL), lambda l, j: (l, 0, j)),
            pl.BlockSpec((1, 1, D_MODEL), lambda l, j: (l, 0, j)),
        ],
        out_specs=pl.BlockSpec((1, bp, D_MODEL), lambda l, j: (l, 0, j)),
        out_shape=jax.ShapeDtypeStruct((DEPTH, bp, 6 * D_MODEL), F32),
        compiler_params=_params(("arbitrary", "arbitrary")),
        name="ada_mod",
    )(c, w_ada, b_ada.reshape(DEPTH, 1, 6 * D_MODEL))


def _inproj_kernel(x_ref, g_ref, sc_ref, sh_ref, w_ref, o_ref, h_scr):
    @pl.when(pl.program_id(1) == 0)
    def _():
        y = _rms(x_ref[...], g_ref[...])
        h_scr[...] = (y * (1.0 + sc_ref[0]) + sh_ref[0]).astype(BF16)

    o_ref[...] = jnp.dot(h_scr[...], w_ref[0], preferred_element_type=F32).astype(BF16)


def _inproj(x, g, sc, sh, w, seq, tm):
    n = x.shape[0]
    per_b = seq // tm
    return pl.pallas_call(
        _inproj_kernel,
        grid=(n // tm, IN_COLS // IN_TILE_N),
        in_specs=[
            pl.BlockSpec((tm, D_MODEL), lambda i, j: (i, 0)),
            pl.BlockSpec((1, D_MODEL), lambda i, j: (0, 0)),
            pl.BlockSpec((1, 1, D_MODEL), lambda i, j: (i // per_b, 0, 0)),
            pl.BlockSpec((1, 1, D_MODEL), lambda i, j: (i // per_b, 0, 0)),
            pl.BlockSpec((1, D_MODEL, IN_TILE_N), lambda i, j: (j, 0, 0)),
        ],
        out_specs=pl.BlockSpec((tm, IN_TILE_N), lambda i, j: (i, j)),
        out_shape=jax.ShapeDtypeStruct((n, IN_COLS), BF16),
        scratch_shapes=[pltpu.VMEM((tm, D_MODEL), BF16)],
        compiler_params=_params(("parallel", "arbitrary")),
        name="in_proj",
    )(x, g, sc, sh, w)


def _rope_kernel(aq_ref, ak_ref, av_ref, cos_ref, sin_ref, gq_ref, gk_ref, bd_ref,
                 qh_ref, ka_ref, kb_ref, vt_ref, nrm_ref):
    tm = aq_ref.shape[0]
    lane = lax.broadcasted_iota(jnp.int32, (tm, LANES), 1)
    first_half = (lane % (2 * ROPE_F)) < ROPE_F
    low = lane < HEAD_DIM
    cos = cos_ref[...]
    sin = sin_ref[...]
    bd = bd_ref[...]

    def norm_rope(x, g):
        xx = x * x
        hi = xx.astype(BF16)
        lo = (xx - hi.astype(F32)).astype(BF16)
        ms = (jnp.dot(hi, bd, preferred_element_type=F32)
              + jnp.dot(lo, bd, preferred_element_type=F32))
        y = x * lax.rsqrt(ms + EPS) * g
        nxt = pltpu.roll(y, LANES - ROPE_F, 1)
        prv = pltpu.roll(y, ROPE_F, 1)
        return y * cos + jnp.where(first_half, nxt, prv) * sin

    def max_sq_norm(y):
        ss = jnp.dot((y * y).astype(BF16), bd, preferred_element_type=F32) * HEAD_DIM
        return jnp.max(ss, axis=0, keepdims=True)

    gq = gq_ref[...]
    scale = HEAD_DIM ** -0.5 * float(np.log2(np.e))
    nrm_ref[...] = jnp.zeros(nrm_ref.shape, F32)
    for j in range(ATT_Q // LANES):
        r = norm_rope(aq_ref[:, j * LANES:(j + 1) * LANES].astype(F32), gq) * scale
        qh_ref[:, (2 * j) * LANES:(2 * j + 1) * LANES] = jnp.where(low, r, 0.0).astype(BF16)
        qh_ref[:, (2 * j + 1) * LANES:(2 * j + 2) * LANES] = jnp.where(low, 0.0, r).astype(BF16)
        nrm_ref[0, j:j + 1, :] = max_sq_norm(r)

    k = norm_rope(ak_ref[...].astype(F32), gk_ref[...])
    nrm_ref[0, NRM_K_ROW:NRM_K_ROW + 1, :] = max_sq_norm(k)
    ka_ref[...] = k.astype(BF16)
    kb_ref[...] = pltpu.roll(k, HEAD_DIM, 1).astype(BF16)
    vt = av_ref[...].astype(F32).T.astype(BF16)
    ones = jnp.ones((VT_ROWS - HEAD_DIM, tm), BF16)
    for g in range(N_KV_HEADS):
        vt_ref[g, 0:HEAD_DIM, :] = vt[g * HEAD_DIM:(g + 1) * HEAD_DIM, :]
        vt_ref[g, HEAD_DIM:VT_ROWS, :] = ones


def _rope(proj, cos_t, sin_t, gq, gk, bd, seq, tm):
    n = proj.shape[0]
    per_b = seq // tm
    kv_spec = pl.BlockSpec((tm, LANES), lambda i: (i, 0))
    vt_spec = pl.BlockSpec((N_KV_HEADS, VT_ROWS, tm), lambda i: (0, 0, i))
    return pl.pallas_call(
        _rope_kernel,
        grid=(n // tm,),
        in_specs=[
            pl.BlockSpec((tm, ATT_Q), lambda i: (i, COL_AQ // ATT_Q)),
            pl.BlockSpec((tm, LANES), lambda i: (i, COL_AK // LANES)),
            pl.BlockSpec((tm, LANES), lambda i: (i, COL_AV // LANES)),
            pl.BlockSpec((tm, LANES), lambda i: (i % per_b, 0)),
            pl.BlockSpec((tm, LANES), lambda i: (i % per_b, 0)),
            pl.BlockSpec((1, LANES), lambda i: (0, 0)),
            pl.BlockSpec((1, LANES), lambda i: (0, 0)),
            pl.BlockSpec((LANES, LANES), lambda i: (0, 0)),
        ],
        out_specs=[pl.BlockSpec((tm, N_HEADS * LANES), lambda i: (i, 0)),
                   kv_spec, kv_spec, vt_spec,
                   pl.BlockSpec((1, SUBLANES, LANES), lambda i: (i, 0, 0))],
        out_shape=[jax.ShapeDtypeStruct((n, N_HEADS * LANES), BF16)]
        + [jax.ShapeDtypeStruct((n, LANES), BF16)] * 2
        + [jax.ShapeDtypeStruct((N_KV_HEADS, VT_ROWS, n), BF16),
           jax.ShapeDtypeStruct((n // tm, SUBLANES, LANES), F32)],
        compiler_params=_params(("parallel",)),
        name="qk_norm_rope",
    )(proj, proj, proj, cos_t, sin_t, gq, gk, bd)


def _attn_kernel(bounded_ref, q_ref, ka_ref, kb_ref, vt_ref, o_ref, m_scr, acc_scr):
    j = pl.program_id(2)
    step = (pl.program_id(0) * pl.num_programs(1) + pl.program_id(1)) * pl.num_programs(2) + j

    @pl.when(j == 0)
    def _():
        m_scr[...] = jnp.full(m_scr.shape, -jnp.inf, F32)
        acc_scr[...] = jnp.zeros(acc_scr.shape, F32)

    def scores(h):
        g, half = h // KV_GROUP, h % 2
        k = (ka_ref if g == half else kb_ref)[...]
        q = q_ref[:, h * LANES:(h + 1) * LANES]
        return lax.dot_general(k, q, (((1,), (1,)), ((), ())), preferred_element_type=F32)

    def all_heads(tile_max):
        st_next = scores(0)
        for h in range(N_HEADS):
            vt = vt_ref[h // KV_GROUP]
            st = st_next
            if h + 1 < N_HEADS:
                st_next = scores(h + 1)
            m_prev = m_scr[h]
            m_new = jnp.maximum(m_prev, tile_max(st))
            alpha = jnp.exp2(m_prev - m_new)
            pt = jnp.exp2((st - m_new).astype(BF16))
            acc_scr[h] = alpha * acc_scr[h] + jnp.dot(vt, pt, preferred_element_type=F32)
            m_scr[h] = m_new

    @pl.when(bounded_ref[step] != 0)
    def _():
        all_heads(lambda st: 0.0)

    @pl.when(bounded_ref[step] == 0)
    def _():
        all_heads(lambda st: jnp.max(st, axis=0, keepdims=True))

    @pl.when(j == pl.num_programs(2) - 1)
    def _():
        for jj in range(N_HEADS // 2):
            pair = []
            for h in (2 * jj, 2 * jj + 1):
                acc = acc_scr[h]
                pair.append(acc[0:HEAD_DIM, :] / acc[HEAD_DIM:HEAD_DIM + 1, :])
            o_ref[:, jj * LANES:(jj + 1) * LANES] = jnp.concatenate(pair, axis=0).T.astype(BF16)


def _attention(qh, ka, kb, vt, nrm, batch, seq, tq, tk):
    n = qh.shape[0]
    nq, nk = seq // tq, seq // tk
    halves = jnp.array([0, HEAD_DIM])
    qn2 = nrm[:, :NRM_K_ROW, :][:, :, halves].reshape(batch, nq, -1, N_HEADS).max(axis=2)
    qn2 = qn2.reshape(batch, nq, 1, N_HEADS)
    kn2 = jnp.repeat(nrm[:, NRM_K_ROW, :][:, halves], KV_GROUP, axis=1)
    kn2 = kn2.reshape(batch, nk, -1, N_HEADS).max(axis=2).reshape(batch, 1, nk, N_HEADS)
    bounded = jnp.all(qn2 * kn2 <= SCORE_BOUND ** 2, axis=-1).astype(jnp.int32).reshape(-1)
    k_spec = pl.BlockSpec((tk, LANES), lambda b, i, j, fl: (b * nk + j, 0))
    vt_spec = pl.BlockSpec((N_KV_HEADS, VT_ROWS, tk), lambda b, i, j, fl: (0, 0, b * nk + j))
    grid_spec = pltpu.PrefetchScalarGridSpec(
        num_scalar_prefetch=1,
        grid=(batch, nq, nk),
        in_specs=[pl.BlockSpec((tq, N_HEADS * LANES), lambda b, i, j, fl: (b * nq + i, 0)),
                  k_spec, k_spec, vt_spec],
        out_specs=pl.BlockSpec((tq, ATT_Q), lambda b, i, j, fl: (b * nq + i, 0)),
        scratch_shapes=[pltpu.VMEM((N_HEADS, 1, tq), F32),
                        pltpu.VMEM((N_HEADS, VT_ROWS, tq), F32)],
    )
    return pl.pallas_call(
        _attn_kernel,
        grid_spec=grid_spec,
        out_shape=jax.ShapeDtypeStruct((n, ATT_Q), BF16),
        compiler_params=_params(("parallel", "parallel", "arbitrary")),
        name="gqa_attention",
    )(bounded, qh, ka, kb, vt)


def _level_ref(b, m, rev):
    c = b.shape[0]
    g = 2 * m
    if g >= SUBLANES:
        b3 = b.reshape(c // g, g, LANES)
        r = m if rev else m - 1
        return jnp.broadcast_to(b3[:, r:r + 1, :], b3.shape).reshape(c, LANES)
    b3 = b.reshape(c // SUBLANES, SUBLANES, LANES)
    sub = lax.broadcasted_iota(jnp.int32, b3.shape, 1)
    out = None
    for p0 in range(0, SUBLANES, g):
        r = p0 + (m if rev else m - 1)
        cand = jnp.broadcast_to(b3[:, r:r + 1, :], b3.shape)
        out = cand if out is None else jnp.where(sub >= p0, cand, out)
    return out.reshape(c, LANES)


def _chunk_cumsum(x, tri):
    hi = x.astype(BF16)
    r1 = x - hi.astype(F32)
    mid = r1.astype(BF16)
    lo = (r1 - mid.astype(F32)).astype(BF16)
    return (jnp.dot(tri, hi, preferred_element_type=F32)
            + jnp.dot(tri, mid, preferred_element_type=F32)
            + jnp.dot(tri, lo, preferred_element_type=F32))


def _neg_abs(z):
    bits = lax.bitcast_convert_type(z, jnp.int32) | jnp.int32(-2 ** 31)
    return lax.bitcast_convert_type(bits, F32)


def _hgrn_kernel(hq_ref, hf_ref, hi_ref, lbl_ref, o_ref, st_scr, *, rev, layer, chunk):
    c = chunk
    n_chunks = hq_ref.shape[0] // c

    @pl.when(pl.program_id(1) == 0)
    def _():
        st_scr[...] = jnp.zeros(st_scr.shape, F32)

    lbl = lbl_ref[...]
    lbe = jnp.exp(lbl - jnp.max(lbl, axis=0, keepdims=True))
    lbp = lbe / jnp.sum(lbe, axis=0, keepdims=True)
    lb_all = jnp.zeros((1, HG_F), F32)
    for l in range(1, layer + 1):
        lb_all = lb_all + lbp[l:l + 1, :]

    row = lax.broadcasted_iota(jnp.int32, (c, LANES), 0)
    row_a = lax.broadcasted_iota(jnp.int32, (c, c), 0)
    col_a = lax.broadcasted_iota(jnp.int32, (c, c), 1)
    tri = jnp.where((row_a <= col_a) if rev else (row_a >= col_a), 1.0, 0.0).astype(BF16)
    lev = 31 - lax.clz(row_a ^ col_a)
    lev = jnp.where((row_a < col_a) if rev else (row_a > col_a), lev, -1)
    q_scale = HG_DK ** -0.5
    log2e = float(np.log2(np.e))
    nt_dims = (((1,), (1,)), ((), ()))

    def body(ci, carry):
        cc = (n_chunks - 1 - ci) if rev else ci
        r0 = pl.multiple_of(cc * c, c)
        heads = range(HG_HEADS)
        cols = [slice(h * LANES, (h + 1) * LANES) for h in heads]
        v = [hi_ref[pl.ds(r0, c), cols[h]] for h in heads]
        q = [_silu(hq_ref[pl.ds(r0, c), cols[h]].astype(F32)) * q_scale for h in heads]
        f = [lb_all[:, cols[h]] + (1.0 - lb_all[:, cols[h]])
             * _sigmoid(hf_ref[pl.ds(r0, c), cols[h]].astype(F32)) for h in heads]
        k = [1.0 - f[h] for h in heads]
        b = [_chunk_cumsum(jnp.log(f[h]) * log2e, tri) for h in heads]
        a = [jnp.zeros((c, c), F32) for h in heads]
        m = c // 2
        while m >= 1:
            upper = (row & m) != 0
            for h in heads:
                e = jnp.exp2(_neg_abs(b[h] - _level_ref(b[h], m, rev)).astype(BF16))
                w = jnp.where(upper, k[h], q[h]) if rev else jnp.where(upper, q[h], k[h])
                w = w.astype(BF16) * e
                al = lax.dot_general(w, w, nt_dims, preferred_element_type=F32)
                a[h] = jnp.where(lev == (m.bit_length() - 1), al, a[h])
            m //= 2
        for h in heads:
            st = st_scr[h]
            o = lax.dot_general((q[h] * jnp.exp2(b[h])).astype(BF16), st.astype(BF16), nt_dims,
                                preferred_element_type=F32)
            diag = jnp.sum(q[h] * k[h], axis=-1, keepdims=True)
            o = (o + jnp.dot(a[h].astype(BF16), v[h], preferred_element_type=F32)
                 + diag * v[h].astype(F32))
            o_ref[pl.ds(r0, c), cols[h]] = o.astype(o_ref.dtype)
            b_end = b[h][0:1, :] if rev else b[h][c - 1:c, :]
            k_end = (k[h] * jnp.exp2(b_end - b[h])).astype(BF16)
            upd = lax.dot_general(v[h], k_end, (((0,), (0,)), ((), ())), preferred_element_type=F32)
            st_scr[h] = st * jnp.exp2(b_end) + upd
        return carry

    lax.fori_loop(0, n_chunks, body, 0, unroll=True)


def _hgrn(proj, lb_logits_dir, batch, seq, tt, rev, layer):
    n = proj.shape[0]
    nt = seq // tt
    col_f = (COL_HFB if rev else COL_HFF) // HG_F

    def rows(b, i):
        return b * nt + ((nt - 1 - i) if rev else i)

    return pl.pallas_call(
        functools.partial(_hgrn_kernel, rev=rev, layer=layer, chunk=min(HG_CHUNK, tt)),
        grid=(batch, nt),
        in_specs=[
            pl.BlockSpec((tt, HG_F), lambda b, i: (rows(b, i), COL_HQ // HG_F)),
            pl.BlockSpec((tt, HG_F), lambda b, i: (rows(b, i), col_f)),
            pl.BlockSpec((tt, HG_F), lambda b, i: (rows(b, i), COL_HI // HG_F)),
            pl.BlockSpec((DEPTH, HG_F), lambda b, i: (0, 0)),
        ],
        out_specs=pl.BlockSpec((tt, HG_F), lambda b, i: (rows(b, i), 0)),
        out_shape=jax.ShapeDtypeStruct((n, HG_F), BF16),
        scratch_shapes=[pltpu.VMEM((HG_HEADS, HG_DV, HG_DK), F32)],
        compiler_params=_params(("parallel", "arbitrary")),
        name="hgrn2_bwd" if rev else "hgrn2_fwd",
    )(proj, proj, proj, lb_logits_dir)


def _mixout_kernel(att_ref, of_ref, ob_ref, hg_ref, ga_ref, gb_ref, x_ref, gt_ref, ghg_ref,
                   gn_ref, wa_ref, wb_ref, wo_ref, o_ref):
    ghg = ghg_ref[...]
    recs = []
    for h in range(HG_HEADS):
        cols = slice(h * LANES, (h + 1) * LANES)
        o = of_ref[:, cols].astype(F32) + ob_ref[:, cols].astype(F32)
        recs.append((_rms(o, ghg) * _silu(hg_ref[:, cols].astype(F32))).astype(BF16))
    rec = jnp.concatenate(recs, axis=1)
    pa = jnp.dot(att_ref[...], wa_ref[...], preferred_element_type=F32)
    pb = jnp.dot(rec, wb_ref[...], preferred_element_type=F32)
    merged = (_sigmoid(ga_ref[...].astype(F32)) * pa + _sigmoid(gb_ref[...].astype(F32)) * pb)
    y = jnp.dot(merged.astype(BF16), wo_ref[...], preferred_element_type=F32)
    o_ref[...] = x_ref[...] + gt_ref[0] * _rms(y, gn_ref[...])


def _mixout(att, o_f, o_b, proj, x, gt, g_hgrn, g_norm, w_a, w_b, w_o, seq, tm):
    n = x.shape[0]
    per_b = seq // tm
    half = pl.BlockSpec((tm, ATT_Q), lambda i: (i, 0))
    const = lambda shape: pl.BlockSpec(shape, lambda i: (0,) * len(shape))
    return pl.pallas_call(
        _mixout_kernel,
        grid=(n // tm,),
        in_specs=[
            half, half, half,
            pl.BlockSpec((tm, HG_F), lambda i: (i, COL_HG // HG_F)),
            pl.BlockSpec((tm, D_MODEL), lambda i: (i, COL_GA // D_MODEL)),
            pl.BlockSpec((tm, D_MODEL), lambda i: (i, COL_GB // D_MODEL)),
            pl.BlockSpec((tm, D_MODEL), lambda i: (i, 0)),
            pl.BlockSpec((1, 1, D_MODEL), lambda i: (i // per_b, 0, 0)),
            const((1, LANES)), const((1, D_MODEL)),
            const((ATT_Q, D_MODEL)), const((HG_F, D_MODEL)), const((D_MODEL, D_MODEL)),
        ],
        out_specs=pl.BlockSpec((tm, D_MODEL), lambda i: (i, 0)),
        out_shape=jax.ShapeDtypeStruct((n, D_MODEL), F32),
        compiler_params=_params(("parallel",)),
        name="mixer_out",
    )(att, o_f, o_b, proj, proj, proj, x, gt, g_hgrn, g_norm, w_a, w_b, w_o)


def _router_kernel(x_ref, g_ref, sc_ref, sh_ref, wr_ref, br_ref, h_ref, sel_ref):
    h = _rms(x_ref[...], g_ref[...]) * (1.0 + sc_ref[0]) + sh_ref[0]
    tm = h.shape[0]
    for j in range(D_MODEL // LANES):
        h_ref[pl.ds(j, tm, stride=SUBLANES), :] = h[:, j * LANES:(j + 1) * LANES]
    logits = jnp.dot(h, wr_ref[...], preferred_element_type=F32,
                     precision=lax.Precision.HIGHEST) + br_ref[...]
    lane = lax.broadcasted_iota(jnp.int32, logits.shape, 1).astype(F32)
    logits = jnp.where(lane < N_EXPERTS, logits, -jnp.inf)
    m1 = jnp.max(logits, axis=-1, keepdims=True)
    i1 = jnp.min(jnp.where(logits == m1, lane, float(LANES)), axis=-1, keepdims=True)
    rest = jnp.where(lane == i1, -jnp.inf, logits)
    m2 = jnp.max(rest, axis=-1, keepdims=True)
    i2 = jnp.min(jnp.where(rest == m2, lane, float(LANES)), axis=-1, keepdims=True)
    e2 = jnp.exp(m2 - m1)
    w1 = 1.0 / (1.0 + e2)
    w2 = e2 / (1.0 + e2)
    sel_ref[...] = (jnp.where(lane == SEL_E1, i1, 0.0) + jnp.where(lane == SEL_E2, i2, 0.0)
                    + jnp.where(lane == SEL_W1, w1, 0.0) + jnp.where(lane == SEL_W2, w2, 0.0))


def _router(x, g, sc, sh, w_r, b_r, seq, tm):
    n = x.shape[0]
    per_b = seq // tm
    return pl.pallas_call(
        _router_kernel,
        grid=(n // tm,),
        in_specs=[
            pl.BlockSpec((tm, D_MODEL), lambda i: (i, 0)),
            pl.BlockSpec((1, D_MODEL), lambda i: (0, 0)),
            pl.BlockSpec((1, 1, D_MODEL), lambda i: (i // per_b, 0, 0)),
            pl.BlockSpec((1, 1, D_MODEL), lambda i: (i // per_b, 0, 0)),
            pl.BlockSpec((D_MODEL, LANES), lambda i: (0, 0)),
            pl.BlockSpec((1, LANES), lambda i: (0, 0)),
        ],
        out_specs=[pl.BlockSpec((tm * SUBLANES, LANES), lambda i: (i, 0)),
                   pl.BlockSpec((tm, LANES), lambda i: (i, 0))],
        out_shape=[jax.ShapeDtypeStruct((n * SUBLANES, LANES), F32),
                   jax.ShapeDtypeStruct((n, LANES), F32)],
        compiler_params=_params(("parallel",)),
        name="router_top2",
    )(x, g, sc, sh, w_r, b_r)


def _route_plan(sel, tm, tmc):
    n = sel.shape[0]
    r = 2 * n
    flat_e = sel[:, :2].astype(jnp.int32).reshape(-1)
    onehot = (flat_e[:, None] == jnp.arange(N_EXPERTS, dtype=jnp.int32)[None, :]).astype(jnp.int32)
    csum = jnp.cumsum(onehot, axis=0)
    rank = jnp.sum(csum * onehot, axis=1) - 1
    counts = csum[-1]
    padded = (counts + tm - 1) // tm * tm
    ends = jnp.cumsum(padded)
    pos = (ends - padded)[flat_e] + rank
    n_tiles = r // tm + N_EXPERTS
    n_valid = (ends[-1] // tm).astype(jnp.int32)
    tile_start = jnp.arange(n_tiles, dtype=jnp.int32) * tm
    texp = jnp.minimum(jnp.searchsorted(ends, tile_start, side="right"), N_EXPERTS - 1)
    texp = jnp.where(tile_start < ends[-1], texp, texp[jnp.maximum(n_valid - 1, 0)])
    order = jnp.argsort(flat_e, stable=True).astype(jnp.int32)
    row_e = jnp.repeat(texp, tm)
    local = jnp.arange(n_tiles * tm, dtype=jnp.int32) - (ends - padded)[row_e]
    sorted_idx = jnp.clip((jnp.cumsum(counts) - counts)[row_e] + local, 0, r - 1)
    src = jnp.where(local < counts[row_e], order[sorted_idx] // 2, 0)
    src = src * SUBLANES
    pos = pos * SUBLANES
    pos_tiles = pos.reshape(n // tmc, tmc, 2).transpose(0, 2, 1).reshape(n // tmc, 2 * tmc)
    return src.reshape(n_tiles, tm), texp.astype(jnp.int32), n_valid.reshape(1), pos_tiles


def _moe_kernel(texp_ref, nv_ref, src_hbm, h_hbm, wa_ref, wb_ref, wo_ref, y_ref,
                hbuf, hb16, acc, idx_smem, gsem, isem, *, tm):
    t = pl.program_id(0)
    f = pl.program_id(1)
    nv = nv_ref[0]
    slot = t % 2
    other = 1 - slot
    n_lane_blocks = D_MODEL // LANES

    def idx_copy(tile, s):
        return pltpu.make_async_copy(src_hbm.at[pl.ds(tile, 1)], idx_smem.at[pl.ds(s, 1)], isem.at[s])

    def issue_row(s, r):
        src_row = pl.multiple_of(idx_smem[s, r], SUBLANES)
        dst_row = pl.multiple_of(r * SUBLANES, SUBLANES)
        pltpu.make_async_copy(h_hbm.at[pl.ds(src_row, SUBLANES)],
                              hbuf.at[s, pl.ds(dst_row, SUBLANES)], gsem.at[s]).start()

    def wait_rows(s):
        pltpu.make_async_copy(h_hbm.at[pl.ds(0, tm * SUBLANES)], hbuf.at[s], gsem.at[s]).wait()

    def swiglu_chunk():
        h = hb16[...]
        a = jnp.dot(h, wa_ref[0, 0], preferred_element_type=F32)
        b = jnp.dot(h, wb_ref[0, 0], preferred_element_type=F32)
        return jnp.dot((_silu(a) * b).astype(BF16), wo_ref[0], preferred_element_type=F32)

    @pl.when((t == 0) & (f == 0))
    def _():
        first = idx_copy(0, 0)
        first.start()
        first.wait()
        lax.fori_loop(0, tm, lambda r, c: (issue_row(0, r), c)[1], 0, unroll=8)

    @pl.when((f == 0) & (t <= nv))
    def _():
        wait_rows(slot)
        for j in range(n_lane_blocks):
            hb16[:, j * LANES:(j + 1) * LANES] = (
                hbuf[slot, pl.ds(j, tm, stride=SUBLANES), :].astype(BF16))

    @pl.when((f == 0) & (t < nv))
    def _():
        idx_copy(t + 1, other).start()
        acc[...] = swiglu_chunk()

    @pl.when((f == 1) & (t < nv))
    def _():
        idx_copy(t + 1, other).wait()
        for r in range(tm):
            issue_row(other, r)
        y = swiglu_chunk()
        for j in range(n_lane_blocks):
            cols = slice(j * LANES, (j + 1) * LANES)
            y_ref[pl.ds(j, tm, stride=SUBLANES), :] = acc[:, cols] + y[:, cols]

    @pl.when((t >= nv) & (f == 0))
    def _():
        y_ref[...] = jnp.zeros(y_ref.shape, F32)


def _moe_experts(h, src, texp, n_valid, w_in, w_out, tm, chunk):
    n_tiles = src.shape[0]
    n_fc = w_in.shape[1] // 2
    assert n_fc == 2, "the kernel's gather schedule and accumulation are written for two chunks"

    def fidx(t, f, nv):
        return jnp.where(t < nv[0], f, n_fc - 1)

    grid_spec = pltpu.PrefetchScalarGridSpec(
        num_scalar_prefetch=2,
        grid=(n_tiles, n_fc),
        in_specs=[
            pl.BlockSpec(memory_space=pl.ANY),
            pl.BlockSpec(memory_space=pl.ANY),
            pl.BlockSpec((1, 1, D_MODEL, chunk), lambda t, f, te, nv: (te[t], fidx(t, f, nv), 0, 0)),
            pl.BlockSpec((1, 1, D_MODEL, chunk),
                         lambda t, f, te, nv: (te[t], n_fc + fidx(t, f, nv), 0, 0)),
            pl.BlockSpec((1, chunk, D_MODEL), lambda t, f, te, nv: (te[t], fidx(t, f, nv), 0)),
        ],
        out_specs=pl.BlockSpec((tm * SUBLANES, LANES), lambda t, f, te, nv: (t, 0)),
        scratch_shapes=[
            pltpu.VMEM((2, tm * SUBLANES, LANES), F32),
            pltpu.VMEM((tm, D_MODEL), BF16),
            pltpu.VMEM((tm, D_MODEL), F32),
            pltpu.SMEM((2, tm), jnp.int32),
            pltpu.SemaphoreType.DMA((2,)),
            pltpu.SemaphoreType.DMA((2,)),
        ],
    )
    return pl.pallas_call(
        functools.partial(_moe_kernel, tm=tm),
        grid_spec=grid_spec,
        out_shape=jax.ShapeDtypeStruct((n_tiles * tm * SUBLANES, LANES), F32),
        compiler_params=_params(("arbitrary", "arbitrary")),
        name="moe_experts",
    )(texp, n_valid, src, h, w_in, w_in, w_out)


def _combine_kernel(pos_hbm, y_hbm, x_ref, sel_ref, gt_ref, gn_ref, o_ref, ybuf, idx_smem, gsem, isem,
                    *, tmc, n_steps):
    i = pl.program_id(0)
    slot = i % 2
    other = 1 - slot
    rows = 2 * tmc

    def idx_copy(tile, s):
        return pltpu.make_async_copy(pos_hbm.at[pl.ds(tile, 1)], idx_smem.at[pl.ds(s, 1)], isem.at[s])

    def issue_rows(s):
        def body(r, carry):
            src_row = pl.multiple_of(idx_smem[s, r], SUBLANES)
            dst_row = pl.multiple_of(r * SUBLANES, SUBLANES)
            pltpu.make_async_copy(y_hbm.at[pl.ds(src_row, SUBLANES)],
                                  ybuf.at[s, pl.ds(dst_row, SUBLANES)], gsem.at[s]).start()
            return carry
        lax.fori_loop(0, rows, body, 0, unroll=8)

    @pl.when(i == 0)
    def _():
        first = idx_copy(0, 0)
        first.start()
        first.wait()
        issue_rows(0)
        if n_steps > 1:
            idx_copy(1, 1).start()

    @pl.when(i + 1 < n_steps)
    def _():
        idx_copy(i + 1, other).wait()
        issue_rows(other)

        @pl.when(i + 2 < n_steps)
        def _():
            idx_copy(i + 2, slot).start()

    pltpu.make_async_copy(y_hbm.at[pl.ds(0, rows * SUBLANES)], ybuf.at[slot], gsem.at[slot]).wait()

    sel = sel_ref[...]
    w1 = sel[:, SEL_W1:SEL_W1 + 1]
    w2 = sel[:, SEL_W2:SEL_W2 + 1]
    second = tmc * SUBLANES
    y = jnp.concatenate(
        [w1 * ybuf[slot, pl.ds(j, tmc, stride=SUBLANES), :]
         + w2 * ybuf[slot, pl.ds(second + j, tmc, stride=SUBLANES), :]
         for j in range(D_MODEL // LANES)], axis=1)
    o_ref[...] = x_ref[...] + gt_ref[0] * _rms(y, gn_ref[...])


def _combine(pos_tiles, y_sorted, x, sel, gt, gn, seq, tmc):
    n = x.shape[0]
    per_b = seq // tmc
    return pl.pallas_call(
        functools.partial(_combine_kernel, tmc=tmc, n_steps=n // tmc),
        grid=(n // tmc,),
        in_specs=[
            pl.BlockSpec(memory_space=pl.ANY),
            pl.BlockSpec(memory_space=pl.ANY),
            pl.BlockSpec((tmc, D_MODEL), lambda i: (i, 0)),
            pl.BlockSpec((tmc, LANES), lambda i: (i, 0)),
            pl.BlockSpec((1, 1, D_MODEL), lambda i: (i // per_b, 0, 0)),
            pl.BlockSpec((1, D_MODEL), lambda i: (0, 0)),
        ],
        out_specs=pl.BlockSpec((tmc, D_MODEL), lambda i: (i, 0)),
        out_shape=jax.ShapeDtypeStruct((n, D_MODEL), F32),
        scratch_shapes=[
            pltpu.VMEM((2, 2 * tmc * SUBLANES, LANES), F32),
            pltpu.SMEM((2, 2 * tmc), jnp.int32),
            pltpu.SemaphoreType.DMA((2,)),
            pltpu.SemaphoreType.DMA((2,)),
        ],
        compiler_params=_params(("arbitrary",)),
        name="moe_combine",
    )(pos_tiles, y_sorted, x, sel, gt, gn)


def _ffn_kernel(x_ref, g_ref, sc_ref, sh_ref, gt_ref, gn_ref, wa_ref, wb_ref, wo_ref,
                o_ref, h_scr, acc_scr):
    f = pl.program_id(1)

    @pl.when(f == 0)
    def _():
        y = _rms(x_ref[...], g_ref[...])
        h_scr[...] = (y * (1.0 + sc_ref[0]) + sh_ref[0]).astype(BF16)
        acc_scr[...] = jnp.zeros(acc_scr.shape, F32)

    h = h_scr[...]
    a = jnp.dot(h, wa_ref[0], preferred_element_type=F32)
    b = jnp.dot(h, wb_ref[0], preferred_element_type=F32)
    acc_scr[...] += jnp.dot((_silu(a) * b).astype(BF16), wo_ref[...], preferred_element_type=F32)

    @pl.when(f == pl.num_programs(1) - 1)
    def _():
        o_ref[...] = x_ref[...] + gt_ref[0] * _rms(acc_scr[...], gn_ref[...])


def _ffn(x, g, sc, sh, gt, gn, w_in, w_out, seq, tm, chunk):
    n = x.shape[0]
    per_b = seq // tm
    n_fc = w_in.shape[0] // 2
    vec = pl.BlockSpec((1, D_MODEL), lambda i, f: (0, 0))
    mod = pl.BlockSpec((1, 1, D_MODEL), lambda i, f: (i // per_b, 0, 0))
    return pl.pallas_call(
        _ffn_kernel,
        grid=(n // tm, n_fc),
        in_specs=[
            pl.BlockSpec((tm, D_MODEL), lambda i, f: (i, 0)), vec, mod, mod, mod, vec,
            pl.BlockSpec((1, D_MODEL, chunk), lambda i, f: (f, 0, 0)),
            pl.BlockSpec((1, D_MODEL, chunk), lambda i, f: (n_fc + f, 0, 0)),
            pl.BlockSpec((chunk, D_MODEL), lambda i, f: (f, 0)),
        ],
        out_specs=pl.BlockSpec((tm, D_MODEL), lambda i, f: (i, 0)),
        out_shape=jax.ShapeDtypeStruct((n, D_MODEL), F32),
        scratch_shapes=[pltpu.VMEM((tm, D_MODEL), BF16), pltpu.VMEM((tm, D_MODEL), F32)],
        compiler_params=_params(("parallel", "arbitrary")),
        name="dense_ffn",
    )(x, g, sc, sh, gt, gn, w_in, w_in, w_out)


def _rope_tables(seq):
    t = jnp.arange(seq, dtype=jnp.int32)
    row = (t // GRID_W).astype(F32)
    col = (t % GRID_W).astype(F32)
    inv = ROPE_THETA ** (-jnp.arange(ROPE_F, dtype=F32) / ROPE_F)
    ang_r = row[:, None] * inv
    ang_c = col[:, None] * inv
    cos = jnp.concatenate([jnp.cos(ang_r)] * 2 + [jnp.cos(ang_c)] * 2, axis=1)
    sin = jnp.concatenate([-jnp.sin(ang_r), jnp.sin(ang_r), -jnp.sin(ang_c), jnp.sin(ang_c)], axis=1)
    return jnp.tile(cos, (1, 2)), jnp.tile(sin, (1, 2))


def _prep_weights(w_in, g_qk, w_branch, w_o, w_ffn_in, w_ffn_out, w_router, b_router,
                  w_moe_in, w_moe_out):
    o_aq, o_ak, o_av, o_hq = 0, ATT_Q, ATT_Q + ATT_KV, ATT_Q + 2 * ATT_KV
    o_ga = o_hq + 5 * HG_F
    w_in_p = jnp.concatenate(
        [w_in[:, :, o_ga:], w_in[:, :, o_hq:o_ga], w_in[:, :, o_aq:o_hq]], axis=2).astype(BF16)
    pad_c = D_FF_PAD - D_FF
    w_fa = jnp.pad(w_ffn_in[:, :, :D_FF], ((0, 0), (0, 0), (0, pad_c)))
    w_fb = jnp.pad(w_ffn_in[:, :, D_FF:], ((0, 0), (0, 0), (0, pad_c)))
    w_ffn_in_p = jnp.concatenate([w_fa, w_fb], axis=2).astype(BF16)
    w_ffn_out_p = jnp.pad(w_ffn_out, ((0, 0), (0, pad_c), (0, 0))).astype(BF16)
    w_r_p = jnp.pad(w_router, ((0, 0), (0, 0), (0, LANES - N_EXPERTS)))
    b_r_p = jnp.pad(b_router, ((0, 0), (0, LANES - N_EXPERTS)))[:, None, :]
    g_q = jnp.tile(g_qk[:, 0, :], (1, LANES // HEAD_DIM))[:, None, :]
    g_k = jnp.tile(g_qk[:, 1, :], (1, LANES // HEAD_DIM))[:, None, :]

    def col_chunks(w, chunk):
        lead, (rows, cols) = w.shape[:-2], w.shape[-2:]
        return jnp.moveaxis(w.reshape(*lead, rows, cols // chunk, chunk), -2, -3)

    return dict(w_in=col_chunks(w_in_p, IN_TILE_N), w_branch=w_branch.astype(BF16),
                w_o=w_o.astype(BF16), w_ffn_in=col_chunks(w_ffn_in_p, FFN_CHUNK_DENSE),
                w_ffn_out=w_ffn_out_p, w_r=w_r_p, b_r=b_r_p,
                w_moe_in=col_chunks(w_moe_in.astype(BF16), FFN_CHUNK_EXPERT),
                w_moe_out=w_moe_out.astype(BF16), g_q=g_q, g_k=g_k)


def _tile(seq, want):
    return min(seq, want)


def _trunk(x3, mods, g_norm, lb_logits, g_hgrn, wts):
    batch, seq, _ = x3.shape
    x = x3.reshape(batch * seq, D_MODEL)
    cos_t, sin_t = _rope_tables(seq)
    idx = np.arange(LANES)
    bd = jnp.asarray((idx[:, None] // HEAD_DIM == idx[None, :] // HEAD_DIM) / HEAD_DIM, BF16)
    tm_big = _tile(seq, 1024)
    tm = _tile(seq, 512)
    for l in range(DEPTH):
        mod = mods[l]
        sh1, sc1, gt1, sh2, sc2, gt2 = [m[:, None, :] for m in jnp.split(mod, 6, axis=-1)]
        gn = g_norm[l][:, None, :]
        proj = _inproj(x, gn[0], sc1, sh1, wts["w_in"][l], seq, tm_big)
        qh, ka, kb, vt, nrm = _rope(proj, cos_t, sin_t, wts["g_q"][l], wts["g_k"][l], bd, seq, tm)
        att = _attention(qh, ka, kb, vt, nrm, batch, seq, tm, min(seq, ATT_KEY_TILE))
        o_f = _hgrn(proj, lb_logits[0], batch, seq, tm, False, l)
        o_b = _hgrn(proj, lb_logits[1], batch, seq, tm, True, l)
        x = _mixout(att, o_f, o_b, proj, x, gt1, g_hgrn[l][None, :], gn[1],
                    wts["w_branch"][l, 0], wts["w_branch"][l, 1], wts["w_o"][l], seq, tm)
        if l % 2 == 0:
            x = _ffn(x, gn[2], sc2, sh2, gt2, gn[3], wts["w_ffn_in"][l // 2],
                     wts["w_ffn_out"][l // 2], seq, tm, FFN_CHUNK_DENSE)
        else:
            h, sel = _router(x, gn[2], sc2, sh2, wts["w_r"][l // 2], wts["b_r"][l // 2], seq, tm)
            moe_tile = min(MOE_TILE, 2 * batch * seq)
            src, texp, n_valid, pos_tiles = _route_plan(sel, moe_tile, tm)
            y_sorted = _moe_experts(h, src, texp, n_valid, wts["w_moe_in"][l // 2],
                                    wts["w_moe_out"][l // 2], moe_tile, FFN_CHUNK_EXPERT)
            x = _combine(pos_tiles, y_sorted, x, sel, gt2, gn[3], seq, tm)
    return x.reshape(batch, seq, D_MODEL)


def kernel(x_prompt, x_sample, c_prompt, c_sample, w_ada, b_ada, g_norm, w_in, g_qk, lb_logits,
           g_hgrn, w_branch, w_o, w_ffn_in, w_ffn_out, w_router, b_router, w_moe_in, w_moe_out):
    wts = _prep_weights(w_in, g_qk, w_branch, w_o, w_ffn_in, w_ffn_out, w_router, b_router,
                        w_moe_in, w_moe_out)
    n_p, n_s = c_prompt.shape[0], c_sample.shape[0]
    c_all = jnp.concatenate([c_prompt, c_sample], axis=0)
    pad = (-c_all.shape[0]) % SUBLANES
    c_all = jnp.pad(c_all, ((0, pad), (0, 0)))
    mods = _ada(c_all, w_ada, b_ada)
    y_p = _trunk(x_prompt, mods[:, :n_p], g_norm, lb_logits, g_hgrn, wts)
    y_s = _trunk(x_sample, mods[:, n_p:n_p + n_s], g_norm, lb_logits, g_hgrn, wts)
    return (y_p, y_s)
```

```python
import functools

import jax
import jax.numpy as jnp
import numpy as np
from jax import lax
from jax.experimental import pallas as pl
from jax.experimental.pallas import tpu as pltpu

F32 = jnp.float32
BF16 = jnp.bfloat16

D_MODEL = 1024
DEPTH = 4
GRID_W = 64
N_HEADS = 8
N_KV_HEADS = 2
HEAD_DIM = 64
KV_GROUP = N_HEADS // N_KV_HEADS
ROPE_F = HEAD_DIM // 4
ROPE_THETA = 10000.0
HG_HEADS = 4
HG_DK = 128
HG_DV = 128
ATT_Q = N_HEADS * HEAD_DIM
ATT_KV = N_KV_HEADS * HEAD_DIM
HG_F = HG_HEADS * HG_DK
D_FF = 2752
N_EXPERTS = 8
D_FF_EXPERT = 3584
EPS = 1e-6

LANES = 128
SUBLANES = 8
VMEM_LIMIT_BYTES = 56 * 1024 * 1024

COL_GA, COL_GB = 0, 1024
COL_HQ, COL_HFF, COL_HFB, COL_HI, COL_HG = 2048, 2560, 3072, 3584, 4096
COL_AQ, COL_AK, COL_AV = 4608, 5120, 5248
IN_COLS = 5376
IN_TILE_N = 1792

D_FF_PAD = 2816
FFN_CHUNK_DENSE = 1408
FFN_CHUNK_EXPERT = 1792
HG_CHUNK = 128
VT_ROWS = HEAD_DIM + 16
NRM_K_ROW = ATT_Q // LANES
SCORE_BOUND = 50.0
ATT_KEY_TILE = 1024
MOE_TILE = 512
SEL_E1, SEL_E2, SEL_W1, SEL_W2 = 0, 1, 2, 3


def _params(sem):
    return pltpu.CompilerParams(dimension_semantics=sem, vmem_limit_bytes=VMEM_LIMIT_BYTES)


def _sigmoid(x):
    return 1.0 / (1.0 + jnp.exp(-x))


def _silu(x):
    return x * _sigmoid(x)


def _rms(x, g):
    ms = jnp.mean(x * x, axis=-1, keepdims=True)
    return x * lax.rsqrt(ms + EPS) * g


def _ada_kernel(c_ref, w_ref, b_ref, o_ref):
    c = c_ref[...]
    o_ref[0] = jnp.dot(_silu(c), w_ref[0], preferred_element_type=F32,
                       precision=lax.Precision.HIGHEST) + b_ref[0]


def _ada(c, w_ada, b_ada):
    bp = c.shape[0]
    n_col = 6 * D_MODEL // D_MODEL
    return pl.pallas_call(
        _ada_kernel,
        grid=(DEPTH, n_col),
        in_specs=[
            pl.BlockSpec((bp, D_MODEL), lambda l, j: (0, 0)),
            pl.BlockSpec((1, D_MODEL, D_MODEL), lambda l, j: (l, 0, j)),
            pl.BlockSpec((1, 1, D_MODEL), lambda l, j: (l, 0, j)),
        ],
        out_specs=pl.BlockSpec((1, bp, D_MODEL), lambda l, j: (l, 0, j)),
        out_shape=jax.ShapeDtypeStruct((DEPTH, bp, 6 * D_MODEL), F32),
        compiler_params=_params(("arbitrary", "arbitrary")),
        name="ada_mod",
    )(c, w_ada, b_ada.reshape(DEPTH, 1, 6 * D_MODEL))


def _inproj_kernel(x_ref, g_ref, sc_ref, sh_ref, w_ref, o_ref, h_scr):
    @pl.when(pl.program_id(1) == 0)
    def _():
        y = _rms(x_ref[...], g_ref[...])
        h_scr[...] = (y * (1.0 + sc_ref[0]) + sh_ref[0]).astype(BF16)

    o_ref[...] = jnp.dot(h_scr[...], w_ref[0], preferred_element_type=F32).astype(BF16)


def _inproj(x, g, sc, sh, w, seq, tm):
    n = x.shape[0]
    per_b = seq // tm
    return pl.pallas_call(
        _inproj_kernel,
        grid=(n // tm, IN_COLS // IN_TILE_N),
        in_specs=[
            pl.BlockSpec((tm, D_MODEL), lambda i, j: (i, 0)),
            pl.BlockSpec((1, D_MODEL), lambda i, j: (0, 0)),
            pl.BlockSpec((1, 1, D_MODEL), lambda i, j: (i // per_b, 0, 0)),
            pl.BlockSpec((1, 1, D_MODEL), lambda i, j: (i // per_b, 0, 0)),
            pl.BlockSpec((1, D_MODEL, IN_TILE_N), lambda i, j: (j, 0, 0)),
        ],
        out_specs=pl.BlockSpec((tm, IN_TILE_N), lambda i, j: (i, j)),
        out_shape=jax.ShapeDtypeStruct((n, IN_COLS), BF16),
        scratch_shapes=[pltpu.VMEM((tm, D_MODEL), BF16)],
        compiler_params=_params(("parallel", "arbitrary")),
        name="in_proj",
    )(x, g, sc, sh, w)


def _rope_kernel(aq_ref, ak_ref, av_ref, cos_ref, sin_ref, gq_ref, gk_ref, bd_ref,
                 qh_ref, ka_ref, kb_ref, vt_ref, nrm_ref):
    tm = aq_ref.shape[0]
    lane = lax.broadcasted_iota(jnp.int32, (tm, LANES), 1)
    first_half = (lane % (2 * ROPE_F)) < ROPE_F
    low = lane < HEAD_DIM
    cos = cos_ref[...]
    sin = sin_ref[...]
    bd = bd_ref[...]

    def norm_rope(x, g):
        xx = x * x
        hi = xx.astype(BF16)
        lo = (xx - hi.astype(F32)).astype(BF16)
        ms = (jnp.dot(hi, bd, preferred_element_type=F32)
              + jnp.dot(lo, bd, preferred_element_type=F32))
        y = x * lax.rsqrt(ms + EPS) * g
        nxt = pltpu.roll(y, LANES - ROPE_F, 1)
        prv = pltpu.roll(y, ROPE_F, 1)
        return y * cos + jnp.where(first_half, nxt, prv) * sin

    def max_sq_norm(y):
        ss = jnp.dot((y * y).astype(BF16), bd, preferred_element_type=F32) * HEAD_DIM
        return jnp.max(ss, axis=0, keepdims=True)

    gq = gq_ref[...]
    scale = HEAD_DIM ** -0.5 * float(np.log2(np.e))
    nrm_ref[...] = jnp.zeros(nrm_ref.shape, F32)
    for j in range(ATT_Q // LANES):
        r = norm_rope(aq_ref[:, j * LANES:(j + 1) * LANES].astype(F32), gq) * scale
        qh_ref[:, (2 * j) * LANES:(2 * j + 1) * LANES] = jnp.where(low, r, 0.0).astype(BF16)
        qh_ref[:, (2 * j + 1) * LANES:(2 * j + 2) * LANES] = jnp.where(low, 0.0, r).astype(BF16)
        nrm_ref[0, j:j + 1, :] = max_sq_norm(r)

    k = norm_rope(ak_ref[...].astype(F32), gk_ref[...])
    nrm_ref[0, NRM_K_ROW:NRM_K_ROW + 1, :] = max_sq_norm(k)
    ka_ref[...] = k.astype(BF16)
    kb_ref[...] = pltpu.roll(k, HEAD_DIM, 1).astype(BF16)
    vt = av_ref[...].astype(F32).T.astype(BF16)
    ones = jnp.ones((VT_ROWS - HEAD_DIM, tm), BF16)
    for g in range(N_KV_HEADS):
        vt_ref[g, 0:HEAD_DIM, :] = vt[g * HEAD_DIM:(g + 1) * HEAD_DIM, :]
        vt_ref[g, HEAD_DIM:VT_ROWS, :] = ones


def _rope(proj, cos_t, sin_t, gq, gk, bd, seq, tm):
    n = proj.shape[0]
    per_b = seq // tm
    kv_spec = pl.BlockSpec((tm, LANES), lambda i: (i, 0))
    vt_spec = pl.BlockSpec((N_KV_HEADS, VT_ROWS, tm), lambda i: (0, 0, i))
    return pl.pallas_call(
        _rope_kernel,
        grid=(n // tm,),
        in_specs=[
            pl.BlockSpec((tm, ATT_Q), lambda i: (i, COL_AQ // ATT_Q)),
            pl.BlockSpec((tm, LANES), lambda i: (i, COL_AK // LANES)),
            pl.BlockSpec((tm, LANES), lambda i: (i, COL_AV // LANES)),
            pl.BlockSpec((tm, LANES), lambda i: (i % per_b, 0)),
            pl.BlockSpec((tm, LANES), lambda i: (i % per_b, 0)),
            pl.BlockSpec((1, LANES), lambda i: (0, 0)),
            pl.BlockSpec((1, LANES), lambda i: (0, 0)),
            pl.BlockSpec((LANES, LANES), lambda i: (0, 0)),
        ],
        out_specs=[pl.BlockSpec((tm, N_HEADS * LANES), lambda i: (i, 0)),
                   kv_spec, kv_spec, vt_spec,
                   pl.BlockSpec((1, SUBLANES, LANES), lambda i: (i, 0, 0))],
        out_shape=[jax.ShapeDtypeStruct((n, N_HEADS * LANES), BF16)]
        + [jax.ShapeDtypeStruct((n, LANES), BF16)] * 2
        + [jax.ShapeDtypeStruct((N_KV_HEADS, VT_ROWS, n), BF16),
           jax.ShapeDtypeStruct((n // tm, SUBLANES, LANES), F32)],
        compiler_params=_params(("parallel",)),
        name="qk_norm_rope",
    )(proj, proj, proj, cos_t, sin_t, gq, gk, bd)


def _attn_kernel(bounded_ref, q_ref, ka_ref, kb_ref, vt_ref, o_ref, m_scr, acc_scr):
    j = pl.program_id(2)
    step = (pl.program_id(0) * pl.num_programs(1) + pl.program_id(1)) * pl.num_programs(2) + j

    @pl.when(j == 0)
    def _():
        m_scr[...] = jnp.full(m_scr.shape, -jnp.inf, F32)
        acc_scr[...] = jnp.zeros(acc_scr.shape, F32)

    def scores(h):
        g, half = h // KV_GROUP, h % 2
        k = (ka_ref if g == half else kb_ref)[...]
        q = q_ref[:, h * LANES:(h + 1) * LANES]
        return lax.dot_general(k, q, (((1,), (1,)), ((), ())), preferred_element_type=F32)

    def all_heads(tile_max):
        st_next = scores(0)
        for h in range(N_HEADS):
            vt = vt_ref[h // KV_GROUP]
            st = st_next
            if h + 1 < N_HEADS:
                st_next = scores(h + 1)
            m_prev = m_scr[h]
            m_new = jnp.maximum(m_prev, tile_max(st))
            alpha = jnp.exp2(m_prev - m_new)
            pt = jnp.exp2((st - m_new).astype(BF16))
            acc_scr[h] = alpha * acc_scr[h] + jnp.dot(vt, pt, preferred_element_type=F32)
            m_scr[h] = m_new

    @pl.when(bounded_ref[step] != 0)
    def _():
        all_heads(lambda st: 0.0)

    @pl.when(bounded_ref[step] == 0)
    def _():
        all_heads(lambda st: jnp.max(st, axis=0, keepdims=True))

    @pl.when(j == pl.num_programs(2) - 1)
    def _():
        for jj in range(N_HEADS // 2):
            pair = []
            for h in (2 * jj, 2 * jj + 1):
                acc = acc_scr[h]
                pair.append(acc[0:HEAD_DIM, :] / acc[HEAD_DIM:HEAD_DIM + 1, :])
            o_ref[:, jj * LANES:(jj + 1) * LANES] = jnp.concatenate(pair, axis=0).T.astype(BF16)


def _attention(qh, ka, kb, vt, nrm, batch, seq, tq, tk):
    n = qh.shape[0]
    nq, nk = seq // tq, seq // tk
    halves = jnp.array([0, HEAD_DIM])
    qn2 = nrm[:, :NRM_K_ROW, :][:, :, halves].reshape(batch, nq, -1, N_HEADS).max(axis=2)
    qn2 = qn2.reshape(batch, nq, 1, N_HEADS)
    kn2 = jnp.repeat(nrm[:, NRM_K_ROW, :][:, halves], KV_GROUP, axis=1)
    kn2 = kn2.reshape(batch, nk, -1, N_HEADS).max(axis=2).reshape(batch, 1, nk, N_HEADS)
    bounded = jnp.all(qn2 * kn2 <= SCORE_BOUND ** 2, axis=-1).astype(jnp.int32).reshape(-1)
    k_spec = pl.BlockSpec((tk, LANES), lambda b, i, j, fl: (b * nk + j, 0))
    vt_spec = pl.BlockSpec((N_KV_HEADS, VT_ROWS, tk), lambda b, i, j, fl: (0, 0, b * nk + j))
    grid_spec = pltpu.PrefetchScalarGridSpec(
        num_scalar_prefetch=1,
        grid=(batch, nq, nk),
        in_specs=[pl.BlockSpec((tq, N_HEADS * LANES), lambda b, i, j, fl: (b * nq + i, 0)),
                  k_spec, k_spec, vt_spec],
        out_specs=pl.BlockSpec((tq, ATT_Q), lambda b, i, j, fl: (b * nq + i, 0)),
        scratch_shapes=[pltpu.VMEM((N_HEADS, 1, tq), F32),
                        pltpu.VMEM((N_HEADS, VT_ROWS, tq), F32)],
    )
    return pl.pallas_call(
        _attn_kernel,
        grid_spec=grid_spec,
        out_shape=jax.ShapeDtypeStruct((n, ATT_Q), BF16),
        compiler_params=_params(("parallel", "parallel", "arbitrary")),
        name="gqa_attention",
    )(bounded, qh, ka, kb, vt)


def _level_ref(b, m, rev):
    c = b.shape[0]
    g = 2 * m
    if g >= SUBLANES:
        b3 = b.reshape(c // g, g, LANES)
        r = m if rev else m - 1
        return jnp.broadcast_to(b3[:, r:r + 1, :], b3.shape).reshape(c, LANES)
    b3 = b.reshape(c // SUBLANES, SUBLANES, LANES)
    sub = lax.broadcasted_iota(jnp.int32, b3.shape, 1)
    out = None
    for p0 in range(0, SUBLANES, g):
        r = p0 + (m if rev else m - 1)
        cand = jnp.broadcast_to(b3[:, r:r + 1, :], b3.shape)
        out = cand if out is None else jnp.where(sub >= p0, cand, out)
    return out.reshape(c, LANES)


def _chunk_cumsum(x, tri):
    hi = x.astype(BF16)
    r1 = x - hi.astype(F32)
    mid = r1.astype(BF16)
    lo = (r1 - mid.astype(F32)).astype(BF16)
    return (jnp.dot(tri, hi, preferred_element_type=F32)
            + jnp.dot(tri, mid, preferred_element_type=F32)
            + jnp.dot(tri, lo, preferred_element_type=F32))


def _neg_abs(z):
    bits = lax.bitcast_convert_type(z, jnp.int32) | jnp.int32(-2 ** 31)
    return lax.bitcast_convert_type(bits, F32)


def _hgrn_kernel(hq_ref, hf_ref, hi_ref, lbl_ref, o_ref, st_scr, *, rev, layer, chunk):
    c = chunk
    n_chunks = hq_ref.shape[0] // c

    @pl.when(pl.program_id(1) == 0)
    def _():
        st_scr[...] = jnp.zeros(st_scr.shape, F32)

    lbl = lbl_ref[...]
    lbe = jnp.exp(lbl - jnp.max(lbl, axis=0, keepdims=True))
    lbp = lbe / jnp.sum(lbe, axis=0, keepdims=True)
    lb_all = jnp.zeros((1, HG_F), F32)
    for l in range(1, layer + 1):
        lb_all = lb_all + lbp[l:l + 1, :]

    row = lax.broadcasted_iota(jnp.int32, (c, LANES), 0)
    row_a = lax.broadcasted_iota(jnp.int32, (c, c), 0)
    col_a = lax.broadcasted_iota(jnp.int32, (c, c), 1)
    tri = jnp.where((row_a <= col_a) if rev else (row_a >= col_a), 1.0, 0.0).astype(BF16)
    lev = 31 - lax.clz(row_a ^ col_a)
    lev = jnp.where((row_a < col_a) if rev else (row_a > col_a), lev, -1)
    q_scale = HG_DK ** -0.5
    log2e = float(np.log2(np.e))
    nt_dims = (((1,), (1,)), ((), ()))

    def body(ci, carry):
        cc = (n_chunks - 1 - ci) if rev else ci
        r0 = pl.multiple_of(cc * c, c)
        heads = range(HG_HEADS)
        cols = [slice(h * LANES, (h + 1) * LANES) for h in heads]
        v = [hi_ref[pl.ds(r0, c), cols[h]] for h in heads]
        q = [_silu(hq_ref[pl.ds(r0, c), cols[h]].astype(F32)) * q_scale for h in heads]
        f = [lb_all[:, cols[h]] + (1.0 - lb_all[:, cols[h]])
             * _sigmoid(hf_ref[pl.ds(r0, c), cols[h]].astype(F32)) for h in heads]
        k = [1.0 - f[h] for h in heads]
        b = [_chunk_cumsum(jnp.log(f[h]) * log2e, tri) for h in heads]
        a = [jnp.zeros((c, c), F32) for h in heads]
        m = c // 2
        while m >= 1:
            upper = (row & m) != 0
            for h in heads:
                e = jnp.exp2(_neg_abs(b[h] - _level_ref(b[h], m, rev)).astype(BF16))
                w = jnp.where(upper, k[h], q[h]) if rev else jnp.where(upper, q[h], k[h])
                w = w.astype(BF16) * e
                al = lax.dot_general(w, w, nt_dims, preferred_element_type=F32)
                a[h] = jnp.where(lev == (m.bit_length() - 1), al, a[h])
            m //= 2
        for h in heads:
            st = st_scr[h]
            o = lax.dot_general((q[h] * jnp.exp2(b[h])).astype(BF16), st.astype(BF16), nt_dims,
                                preferred_element_type=F32)
            diag = jnp.sum(q[h] * k[h], axis=-1, keepdims=True)
            o = (o + jnp.dot(a[h].astype(BF16), v[h], preferred_element_type=F32)
                 + diag * v[h].astype(F32))
            o_ref[pl.ds(r0, c), cols[h]] = o.astype(o_ref.dtype)
            b_end = b[h][0:1, :] if rev else b[h][c - 1:c, :]
            k_end = (k[h] * jnp.exp2(b_end - b[h])).astype(BF16)
            upd = lax.dot_general(v[h], k_end, (((0,), (0,)), ((), ())), preferred_element_type=F32)
            st_scr[h] = st * jnp.exp2(b_end) + upd
        return carry

    lax.fori_loop(0, n_chunks, body, 0, unroll=True)


def _hgrn(proj, lb_logits_dir, batch, seq, tt, rev, layer):
    n = proj.shape[0]
    nt = seq // tt
    col_f = (COL_HFB if rev else COL_HFF) // HG_F

    def rows(b, i):
        return b * nt + ((nt - 1 - i) if rev else i)

    return pl.pallas_call(
        functools.partial(_hgrn_kernel, rev=rev, layer=layer, chunk=min(HG_CHUNK, tt)),
        grid=(batch, nt),
        in_specs=[
            pl.BlockSpec((tt, HG_F), lambda b, i: (rows(b, i), COL_HQ // HG_F)),
            pl.BlockSpec((tt, HG_F), lambda b, i: (rows(b, i), col_f)),
            pl.BlockSpec((tt, HG_F), lambda b, i: (rows(b, i), COL_HI // HG_F)),
            pl.BlockSpec((DEPTH, HG_F), lambda b, i: (0, 0)),
        ],
        out_specs=pl.BlockSpec((tt, HG_F), lambda b, i: (rows(b, i), 0)),
        out_shape=jax.ShapeDtypeStruct((n, HG_F), BF16),
        scratch_shapes=[pltpu.VMEM((HG_HEADS, HG_DV, HG_DK), F32)],
        compiler_params=_params(("parallel", "arbitrary")),
        name="hgrn2_bwd" if rev else "hgrn2_fwd",
    )(proj, proj, proj, lb_logits_dir)


def _mixout_kernel(att_ref, of_ref, ob_ref, hg_ref, ga_ref, gb_ref, x_ref, gt_ref, ghg_ref,
                   gn_ref, wa_ref, wb_ref, wo_ref, o_ref):
    ghg = ghg_ref[...]
    recs = []
    for h in range(HG_HEADS):
        cols = slice(h * LANES, (h + 1) * LANES)
        o = of_ref[:, cols].astype(F32) + ob_ref[:, cols].astype(F32)
        recs.append((_rms(o, ghg) * _silu(hg_ref[:, cols].astype(F32))).astype(BF16))
    rec = jnp.concatenate(recs, axis=1)
    pa = jnp.dot(att_ref[...], wa_ref[...], preferred_element_type=F32)
    pb = jnp.dot(rec, wb_ref[...], preferred_element_type=F32)
    merged = (_sigmoid(ga_ref[...].astype(F32)) * pa + _sigmoid(gb_ref[...].astype(F32)) * pb)
    y = jnp.dot(merged.astype(BF16), wo_ref[...], preferred_element_type=F32)
    o_ref[...] = x_ref[...] + gt_ref[0] * _rms(y, gn_ref[...])


def _mixout(att, o_f, o_b, proj, x, gt, g_hgrn, g_norm, w_a, w_b, w_o, seq, tm):
    n = x.shape[0]
    per_b = seq // tm
    half = pl.BlockSpec((tm, ATT_Q), lambda i: (i, 0))
    const = lambda shape: pl.BlockSpec(shape, lambda i: (0,) * len(shape))
    return pl.pallas_call(
        _mixout_kernel,
        grid=(n // tm,),
        in_specs=[
            half, half, half,
            pl.BlockSpec((tm, HG_F), lambda i: (i, COL_HG // HG_F)),
            pl.BlockSpec((tm, D_MODEL), lambda i: (i, COL_GA // D_MODEL)),
            pl.BlockSpec((tm, D_MODEL), lambda i: (i, COL_GB // D_MODEL)),
            pl.BlockSpec((tm, D_MODEL), lambda i: (i, 0)),
            pl.BlockSpec((1, 1, D_MODEL), lambda i: (i // per_b, 0, 0)),
            const((1, LANES)), const((1, D_MODEL)),
            const((ATT_Q, D_MODEL)), const((HG_F, D_MODEL)), const((D_MODEL, D_MODEL)),
        ],
        out_specs=pl.BlockSpec((tm, D_MODEL), lambda i: (i, 0)),
        out_shape=jax.ShapeDtypeStruct((n, D_MODEL), F32),
        compiler_params=_params(("parallel",)),
        name="mixer_out",
    )(att, o_f, o_b, proj, proj, proj, x, gt, g_hgrn, g_norm, w_a, w_b, w_o)


def _router_kernel(x_ref, g_ref, sc_ref, sh_ref, wr_ref, br_ref, h_ref, sel_ref):
    h = _rms(x_ref[...], g_ref[...]) * (1.0 + sc_ref[0]) + sh_ref[0]
    tm = h.shape[0]
    for j in range(D_MODEL // LANES):
        h_ref[pl.ds(j, tm, stride=SUBLANES), :] = h[:, j * LANES:(j + 1) * LANES]
    logits = jnp.dot(h, wr_ref[...], preferred_element_type=F32,
                     precision=lax.Precision.HIGHEST) + br_ref[...]
    lane = lax.broadcasted_iota(jnp.int32, logits.shape, 1).astype(F32)
    logits = jnp.where(lane < N_EXPERTS, logits, -jnp.inf)
    m1 = jnp.max(logits, axis=-1, keepdims=True)
    i1 = jnp.min(jnp.where(logits == m1, lane, float(LANES)), axis=-1, keepdims=True)
    rest = jnp.where(lane == i1, -jnp.inf, logits)
    m2 = jnp.max(rest, axis=-1, keepdims=True)
    i2 = jnp.min(jnp.where(rest == m2, lane, float(LANES)), axis=-1, keepdims=True)
    e2 = jnp.exp(m2 - m1)
    w1 = 1.0 / (1.0 + e2)
    w2 = e2 / (1.0 + e2)
    sel_ref[...] = (jnp.where(lane == SEL_E1, i1, 0.0) + jnp.where(lane == SEL_E2, i2, 0.0)
                    + jnp.where(lane == SEL_W1, w1, 0.0) + jnp.where(lane == SEL_W2, w2, 0.0))


def _router(x, g, sc, sh, w_r, b_r, seq, tm):
    n = x.shape[0]
    per_b = seq // tm
    return pl.pallas_call(
        _router_kernel,
        grid=(n // tm,),
        in_specs=[
            pl.BlockSpec((tm, D_MODEL), lambda i: (i, 0)),
            pl.BlockSpec((1, D_MODEL), lambda i: (0, 0)),
            pl.BlockSpec((1, 1, D_MODEL), lambda i: (i // per_b, 0, 0)),
            pl.BlockSpec((1, 1, D_MODEL), lambda i: (i // per_b, 0, 0)),
            pl.BlockSpec((D_MODEL, LANES), lambda i: (0, 0)),
            pl.BlockSpec((1, LANES), lambda i: (0, 0)),
        ],
        out_specs=[pl.BlockSpec((tm * SUBLANES, LANES), lambda i: (i, 0)),
                   pl.BlockSpec((tm, LANES), lambda i: (i, 0))],
        out_shape=[jax.ShapeDtypeStruct((n * SUBLANES, LANES), F32),
                   jax.ShapeDtypeStruct((n, LANES), F32)],
        compiler_params=_params(("parallel",)),
        name="router_top2",
    )(x, g, sc, sh, w_r, b_r)


def _route_plan(sel, tm, tmc):
    n = sel.shape[0]
    r = 2 * n
    flat_e = sel[:, :2].astype(jnp.int32).reshape(-1)
    onehot = (flat_e[:, None] == jnp.arange(N_EXPERTS, dtype=jnp.int32)[None, :]).astype(jnp.int32)
    csum = jnp.cumsum(onehot, axis=0)
    rank = jnp.sum(csum * onehot, axis=1) - 1
    counts = csum[-1]
    padded = (counts + tm - 1) // tm * tm
    ends = jnp.cumsum(padded)
    pos = (ends - padded)[flat_e] + rank
    n_tiles = r // tm + N_EXPERTS
    n_valid = (ends[-1] // tm).astype(jnp.int32)
    tile_start = jnp.arange(n_tiles, dtype=jnp.int32) * tm
    texp = jnp.minimum(jnp.searchsorted(ends, tile_start, side="right"), N_EXPERTS - 1)
    texp = jnp.where(tile_start < ends[-1], texp, texp[jnp.maximum(n_valid - 1, 0)])
    order = jnp.argsort(flat_e, stable=True).astype(jnp.int32)
    row_e = jnp.repeat(texp, tm)
    local = jnp.arange(n_tiles * tm, dtype=jnp.int32) - (ends - padded)[row_e]
    sorted_idx = jnp.clip((jnp.cumsum(counts) - counts)[row_e] + local, 0, r - 1)
    src = jnp.where(local < counts[row_e], order[sorted_idx] // 2, 0)
    src = src * SUBLANES
    pos = pos * SUBLANES
    pos_tiles = pos.reshape(n // tmc, tmc, 2).transpose(0, 2, 1).reshape(n // tmc, 2 * tmc)
    return src.reshape(n_tiles, tm), texp.astype(jnp.int32), n_valid.reshape(1), pos_tiles


def _moe_kernel(texp_ref, nv_ref, src_hbm, h_hbm, wa_ref, wb_ref, wo_ref, y_ref,
                hbuf, hb16, acc, idx_smem, gsem, isem, *, tm):
    t = pl.program_id(0)
    f = pl.program_id(1)
    nv = nv_ref[0]
    slot = t % 2
    other = 1 - slot
    n_lane_blocks = D_MODEL // LANES

    def idx_copy(tile, s):
        return pltpu.make_async_copy(src_hbm.at[pl.ds(tile, 1)], idx_smem.at[pl.ds(s, 1)], isem.at[s])

    def issue_row(s, r):
        src_row = pl.multiple_of(idx_smem[s, r], SUBLANES)
        dst_row = pl.multiple_of(r * SUBLANES, SUBLANES)
        pltpu.make_async_copy(h_hbm.at[pl.ds(src_row, SUBLANES)],
                              hbuf.at[s, pl.ds(dst_row, SUBLANES)], gsem.at[s]).start()

    def wait_rows(s):
        pltpu.make_async_copy(h_hbm.at[pl.ds(0, tm * SUBLANES)], hbuf.at[s], gsem.at[s]).wait()

    def swiglu_chunk():
        h = hb16[...]
        a = jnp.dot(h, wa_ref[0, 0], preferred_element_type=F32)
        b = jnp.dot(h, wb_ref[0, 0], preferred_element_type=F32)
        return jnp.dot((_silu(a) * b).astype(BF16), wo_ref[0], preferred_element_type=F32)

    @pl.when((t == 0) & (f == 0))
    def _():
        first = idx_copy(0, 0)
        first.start()
        first.wait()
        lax.fori_loop(0, tm, lambda r, c: (issue_row(0, r), c)[1], 0, unroll=8)
        idx_copy(1, 1).start()

    @pl.when((f == 0) & (t <= nv))
    def _():
        wait_rows(slot)
        for j in range(n_lane_blocks):
            hb16[:, j * LANES:(j + 1) * LANES] = (
                hbuf[slot, pl.ds(j, tm, stride=SUBLANES), :].astype(BF16))

    @pl.when((f == 0) & (t < nv))
    def _():
        idx_copy(t + 1, other).wait()
        for r in range(tm):
            issue_row(other, r)
        acc[...] = swiglu_chunk()

    @pl.when((f == 1) & (t < nv))
    def _():
        @pl.when(t + 1 < nv)
        def _():
            idx_copy(t + 2, slot).start()

        y = swiglu_chunk()
        for j in range(n_lane_blocks):
            cols = slice(j * LANES, (j + 1) * LANES)
            y_ref[pl.ds(j, tm, stride=SUBLANES), :] = acc[:, cols] + y[:, cols]

    @pl.when((t >= nv) & (f == 0))
    def _():
        y_ref[...] = jnp.zeros(y_ref.shape, F32)


def _moe_experts(h, src, texp, n_valid, w_in, w_out, tm, chunk):
    n_tiles = src.shape[0]
    n_fc = w_in.shape[1] // 2
    assert n_fc == 2, "the kernel's gather schedule and accumulation are written for two chunks"

    def fidx(t, f, nv):
        return jnp.where(t < nv[0], f, n_fc - 1)

    grid_spec = pltpu.PrefetchScalarGridSpec(
        num_scalar_prefetch=2,
        grid=(n_tiles, n_fc),
        in_specs=[
            pl.BlockSpec(memory_space=pl.ANY),
            pl.BlockSpec(memory_space=pl.ANY),
            pl.BlockSpec((1, 1, D_MODEL, chunk), lambda t, f, te, nv: (te[t], fidx(t, f, nv), 0, 0)),
            pl.BlockSpec((1, 1, D_MODEL, chunk),
                         lambda t, f, te, nv: (te[t], n_fc + fidx(t, f, nv), 0, 0)),
            pl.BlockSpec((1, chunk, D_MODEL), lambda t, f, te, nv: (te[t], fidx(t, f, nv), 0)),
        ],
        out_specs=pl.BlockSpec((tm * SUBLANES, LANES), lambda t, f, te, nv: (t, 0)),
        scratch_shapes=[
            pltpu.VMEM((2, tm * SUBLANES, LANES), F32),
            pltpu.VMEM((tm, D_MODEL), BF16),
            pltpu.VMEM((tm, D_MODEL), F32),
            pltpu.SMEM((2, tm), jnp.int32),
            pltpu.SemaphoreType.DMA((2,)),
            pltpu.SemaphoreType.DMA((2,)),
        ],
    )
    return pl.pallas_call(
        functools.partial(_moe_kernel, tm=tm),
        grid_spec=grid_spec,
        out_shape=jax.ShapeDtypeStruct((n_tiles * tm * SUBLANES, LANES), F32),
        compiler_params=_params(("arbitrary", "arbitrary")),
        name="moe_experts",
    )(texp, n_valid, src, h, w_in, w_in, w_out)


def _combine_kernel(pos_hbm, y_hbm, x_ref, sel_ref, gt_ref, gn_ref, o_ref, ybuf, idx_smem, gsem, isem,
                    *, tmc, n_steps):
    i = pl.program_id(0)
    slot = i % 2
    other = 1 - slot
    rows = 2 * tmc

    def idx_copy(tile, s):
        return pltpu.make_async_copy(pos_hbm.at[pl.ds(tile, 1)], idx_smem.at[pl.ds(s, 1)], isem.at[s])

    def issue_rows(s):
        def body(r, carry):
            src_row = pl.multiple_of(idx_smem[s, r], SUBLANES)
            dst_row = pl.multiple_of(r * SUBLANES, SUBLANES)
            pltpu.make_async_copy(y_hbm.at[pl.ds(src_row, SUBLANES)],
                                  ybuf.at[s, pl.ds(dst_row, SUBLANES)], gsem.at[s]).start()
            return carry
        lax.fori_loop(0, rows, body, 0, unroll=8)

    @pl.when(i == 0)
    def _():
        first = idx_copy(0, 0)
        first.start()
        first.wait()
        issue_rows(0)
        if n_steps > 1:
            idx_copy(1, 1).start()

    @pl.when(i + 1 < n_steps)
    def _():
        idx_copy(i + 1, other).wait()
        issue_rows(other)

        @pl.when(i + 2 < n_steps)
        def _():
            idx_copy(i + 2, slot).start()

    pltpu.make_async_copy(y_hbm.at[pl.ds(0, rows * SUBLANES)], ybuf.at[slot], gsem.at[slot]).wait()

    sel = sel_ref[...]
    w1 = sel[:, SEL_W1:SEL_W1 + 1]
    w2 = sel[:, SEL_W2:SEL_W2 + 1]
    second = tmc * SUBLANES
    y = jnp.concatenate(
        [w1 * ybuf[slot, pl.ds(j, tmc, stride=SUBLANES), :]
         + w2 * ybuf[slot, pl.ds(second + j, tmc, stride=SUBLANES), :]
         for j in range(D_MODEL // LANES)], axis=1)
    o_ref[...] = x_ref[...] + gt_ref[0] * _rms(y, gn_ref[...])


def _combine(pos_tiles, y_sorted, x, sel, gt, gn, seq, tmc):
    n = x.shape[0]
    per_b = seq // tmc
    return pl.pallas_call(
        functools.partial(_combine_kernel, tmc=tmc, n_steps=n // tmc),
        grid=(n // tmc,),
        in_specs=[
            pl.BlockSpec(memory_space=pl.ANY),
            pl.BlockSpec(memory_space=pl.ANY),
            pl.BlockSpec((tmc, D_MODEL), lambda i: (i, 0)),
            pl.BlockSpec((tmc, LANES), lambda i: (i, 0)),
            pl.BlockSpec((1, 1, D_MODEL), lambda i: (i // per_b, 0, 0)),
            pl.BlockSpec((1, D_MODEL), lambda i: (0, 0)),
        ],
        out_specs=pl.BlockSpec((tmc, D_MODEL), lambda i: (i, 0)),
        out_shape=jax.ShapeDtypeStruct((n, D_MODEL), F32),
        scratch_shapes=[
            pltpu.VMEM((2, 2 * tmc * SUBLANES, LANES), F32),
            pltpu.SMEM((2, 2 * tmc), jnp.int32),
            pltpu.SemaphoreType.DMA((2,)),
            pltpu.SemaphoreType.DMA((2,)),
        ],
        compiler_params=_params(("arbitrary",)),
        name="moe_combine",
    )(pos_tiles, y_sorted, x, sel, gt, gn)


def _ffn_kernel(x_ref, g_ref, sc_ref, sh_ref, gt_ref, gn_ref, wa_ref, wb_ref, wo_ref,
                o_ref, h_scr, acc_scr):
    f = pl.program_id(1)

    @pl.when(f == 0)
    def _():
        y = _rms(x_ref[...], g_ref[...])
        h_scr[...] = (y * (1.0 + sc_ref[0]) + sh_ref[0]).astype(BF16)
        acc_scr[...] = jnp.zeros(acc_scr.shape, F32)

    h = h_scr[...]
    a = jnp.dot(h, wa_ref[0], preferred_element_type=F32)
    b = jnp.dot(h, wb_ref[0], preferred_element_type=F32)
    acc_scr[...] += jnp.dot((_silu(a) * b).astype(BF16), wo_ref[...], preferred_element_type=F32)

    @pl.when(f == pl.num_programs(1) - 1)
    def _():
        o_ref[...] = x_ref[...] + gt_ref[0] * _rms(acc_scr[...], gn_ref[...])


def _ffn(x, g, sc, sh, gt, gn, w_in, w_out, seq, tm, chunk):
    n = x.shape[0]
    per_b = seq // tm
    n_fc = w_in.shape[0] // 2
    vec = pl.BlockSpec((1, D_MODEL), lambda i, f: (0, 0))
    mod = pl.BlockSpec((1, 1, D_MODEL), lambda i, f: (i // per_b, 0, 0))
    return pl.pallas_call(
        _ffn_kernel,
        grid=(n // tm, n_fc),
        in_specs=[
            pl.BlockSpec((tm, D_MODEL), lambda i, f: (i, 0)), vec, mod, mod, mod, vec,
            pl.BlockSpec((1, D_MODEL, chunk), lambda i, f: (f, 0, 0)),
            pl.BlockSpec((1, D_MODEL, chunk), lambda i, f: (n_fc + f, 0, 0)),
            pl.BlockSpec((chunk, D_MODEL), lambda i, f: (f, 0)),
        ],
        out_specs=pl.BlockSpec((tm, D_MODEL), lambda i, f: (i, 0)),
        out_shape=jax.ShapeDtypeStruct((n, D_MODEL), F32),
        scratch_shapes=[pltpu.VMEM((tm, D_MODEL), BF16), pltpu.VMEM((tm, D_MODEL), F32)],
        compiler_params=_params(("parallel", "arbitrary")),
        name="dense_ffn",
    )(x, g, sc, sh, gt, gn, w_in, w_in, w_out)


def _rope_tables(seq):
    t = jnp.arange(seq, dtype=jnp.int32)
    row = (t // GRID_W).astype(F32)
    col = (t % GRID_W).astype(F32)
    inv = ROPE_THETA ** (-jnp.arange(ROPE_F, dtype=F32) / ROPE_F)
    ang_r = row[:, None] * inv
    ang_c = col[:, None] * inv
    cos = jnp.concatenate([jnp.cos(ang_r)] * 2 + [jnp.cos(ang_c)] * 2, axis=1)
    sin = jnp.concatenate([-jnp.sin(ang_r), jnp.sin(ang_r), -jnp.sin(ang_c), jnp.sin(ang_c)], axis=1)
    return jnp.tile(cos, (1, 2)), jnp.tile(sin, (1, 2))


def _prep_weights(w_in, g_qk, w_branch, w_o, w_ffn_in, w_ffn_out, w_router, b_router,
                  w_moe_in, w_moe_out):
    o_aq, o_ak, o_av, o_hq = 0, ATT_Q, ATT_Q + ATT_KV, ATT_Q + 2 * ATT_KV
    o_ga = o_hq + 5 * HG_F
    w_in_p = jnp.concatenate(
        [w_in[:, :, o_ga:], w_in[:, :, o_hq:o_ga], w_in[:, :, o_aq:o_hq]], axis=2).astype(BF16)
    pad_c = D_FF_PAD - D_FF
    w_fa = jnp.pad(w_ffn_in[:, :, :D_FF], ((0, 0), (0, 0), (0, pad_c)))
    w_fb = jnp.pad(w_ffn_in[:, :, D_FF:], ((0, 0), (0, 0), (0, pad_c)))
    w_ffn_in_p = jnp.concatenate([w_fa, w_fb], axis=2).astype(BF16)
    w_ffn_out_p = jnp.pad(w_ffn_out, ((0, 0), (0, pad_c), (0, 0))).astype(BF16)
    w_r_p = jnp.pad(w_router, ((0, 0), (0, 0), (0, LANES - N_EXPERTS)))
    b_r_p = jnp.pad(b_router, ((0, 0), (0, LANES - N_EXPERTS)))[:, None, :]
    g_q = jnp.tile(g_qk[:, 0, :], (1, LANES // HEAD_DIM))[:, None, :]
    g_k = jnp.tile(g_qk[:, 1, :], (1, LANES // HEAD_DIM))[:, None, :]

    def col_chunks(w, chunk):
        lead, (rows, cols) = w.shape[:-2], w.shape[-2:]
        return jnp.moveaxis(w.reshape(*lead, rows, cols // chunk, chunk), -2, -3)

    return dict(w_in=col_chunks(w_in_p, IN_TILE_N), w_branch=w_branch.astype(BF16),
                w_o=w_o.astype(BF16), w_ffn_in=col_chunks(w_ffn_in_p, FFN_CHUNK_DENSE),
                w_ffn_out=w_ffn_out_p, w_r=w_r_p, b_r=b_r_p,
                w_moe_in=col_chunks(w_moe_in.astype(BF16), FFN_CHUNK_EXPERT),
                w_moe_out=w_moe_out.astype(BF16), g_q=g_q, g_k=g_k)


def _tile(seq, want):
    return min(seq, want)


def _trunk(x3, mods, g_norm, lb_logits, g_hgrn, wts):
    batch, seq, _ = x3.shape
    x = x3.reshape(batch * seq, D_MODEL)
    cos_t, sin_t = _rope_tables(seq)
    idx = np.arange(LANES)
    bd = jnp.asarray((idx[:, None] // HEAD_DIM == idx[None, :] // HEAD_DIM) / HEAD_DIM, BF16)
    tm_big = _tile(seq, 1024)
    tm = _tile(seq, 512)
    for l in range(DEPTH):
        mod = mods[l]
        sh1, sc1, gt1, sh2, sc2, gt2 = [m[:, None, :] for m in jnp.split(mod, 6, axis=-1)]
        gn = g_norm[l][:, None, :]
        proj = _inproj(x, gn[0], sc1, sh1, wts["w_in"][l], seq, tm_big)
        qh, ka, kb, vt, nrm = _rope(proj, cos_t, sin_t, wts["g_q"][l], wts["g_k"][l], bd, seq, tm)
        att = _attention(qh, ka, kb, vt, nrm, batch, seq, tm, min(seq, ATT_KEY_TILE))
        o_f = _hgrn(proj, lb_logits[0], batch, seq, tm, False, l)
        o_b = _hgrn(proj, lb_logits[1], batch, seq, tm, True, l)
        x = _mixout(att, o_f, o_b, proj, x, gt1, g_hgrn[l][None, :], gn[1],
                    wts["w_branch"][l, 0], wts["w_branch"][l, 1], wts["w_o"][l], seq, tm)
        if l % 2 == 0:
            x = _ffn(x, gn[2], sc2, sh2, gt2, gn[3], wts["w_ffn_in"][l // 2],
                     wts["w_ffn_out"][l // 2], seq, tm, FFN_CHUNK_DENSE)
        else:
            h, sel = _router(x, gn[2], sc2, sh2, wts["w_r"][l // 2], wts["b_r"][l // 2], seq, tm)
            moe_tile = min(MOE_TILE, 2 * batch * seq)
            src, texp, n_valid, pos_tiles = _route_plan(sel, moe_tile, tm)
            y_sorted = _moe_experts(h, src, texp, n_valid, wts["w_moe_in"][l // 2],
                                    wts["w_moe_out"][l // 2], moe_tile, FFN_CHUNK_EXPERT)
            x = _combine(pos_tiles, y_sorted, x, sel, gt2, gn[3], seq, tm)
    return x.reshape(batch, seq, D_MODEL)


def kernel(x_prompt, x_sample, c_prompt, c_sample, w_ada, b_ada, g_norm, w_in, g_qk, lb_logits,
           g_hgrn, w_branch, w_o, w_ffn_in, w_ffn_out, w_router, b_router, w_moe_in, w_moe_out):
    wts = _prep_weights(w_in, g_qk, w_branch, w_o, w_ffn_in, w_ffn_out, w_router, b_router,
                        w_moe_in, w_moe_out)
    n_p, n_s = c_prompt.shape[0], c_sample.shape[0]
    c_all = jnp.concatenate([c_prompt, c_sample], axis=0)
    pad = (-c_all.shape[0]) % SUBLANES
    c_all = jnp.pad(c_all, ((0, pad), (0, 0)))
    mods = _ada(c_all, w_ada, b_ada)
    y_p = _trunk(x_prompt, mods[:, :n_p], g_norm, lb_logits, g_hgrn, wts)
    y_s = _trunk(x_sample, mods[:, n_p:n_p + n_s], g_norm, lb_logits, g_hgrn, wts)
    return (y_p, y_s)
```

```python
import functools

import jax
import jax.numpy as jnp
import numpy as np
from jax import lax
from jax.experimental import pallas as pl
from jax.experimental.pallas import tpu as pltpu

F32 = jnp.float32
BF16 = jnp.bfloat16

D_MODEL = 1024
DEPTH = 4
GRID_W = 64
N_HEADS = 8
N_KV_HEADS = 2
HEAD_DIM = 64
KV_GROUP = N_HEADS // N_KV_HEADS
ROPE_F = HEAD_DIM // 4
ROPE_THETA = 10000.0
HG_HEADS = 4
HG_DK = 128
HG_DV = 128
ATT_Q = N_HEADS * HEAD_DIM
ATT_KV = N_KV_HEADS * HEAD_DIM
HG_F = HG_HEADS * HG_DK
D_FF = 2752
N_EXPERTS = 8
D_FF_EXPERT = 3584
EPS = 1e-6

LANES = 128
SUBLANES = 8
VMEM_LIMIT_BYTES = 56 * 1024 * 1024

COL_GA, COL_GB = 0, 1024
COL_HQ, COL_HFF, COL_HFB, COL_HI, COL_HG = 2048, 2560, 3072, 3584, 4096
COL_AQ, COL_AK, COL_AV = 4608, 5120, 5248
IN_COLS = 5376
IN_TILE_N = 1792

D_FF_PAD = 2816
FFN_CHUNK_DENSE = 1408
FFN_CHUNK_EXPERT = 1792
HG_CHUNK = 128
VT_ROWS = HEAD_DIM + 16
NRM_K_ROW = ATT_Q // LANES
SCORE_BOUND = 50.0
ATT_KEY_TILE = 1024
MOE_TILE = 512
SEL_E1, SEL_E2, SEL_W1, SEL_W2 = 0, 1, 2, 3


def _params(sem):
    return pltpu.CompilerParams(dimension_semantics=sem, vmem_limit_bytes=VMEM_LIMIT_BYTES)


def _sigmoid(x):
    return 1.0 / (1.0 + jnp.exp(-x))


def _silu(x):
    return x * _sigmoid(x)


def _rms(x, g):
    ms = jnp.mean(x * x, axis=-1, keepdims=True)
    return x * lax.rsqrt(ms + EPS) * g


def _ada_kernel(c_ref, w_ref, b_ref, o_ref):
    c = c_ref[...]
    o_ref[0] = jnp.dot(_silu(c), w_ref[0], preferred_element_type=F32,
                       precision=lax.Precision.HIGHEST) + b_ref[0]


def _ada(c, w_ada, b_ada):
    bp = c.shape[0]
    n_col = 6 * D_MODEL // D_MODEL
    return pl.pallas_call(
        _ada_kernel,
        grid=(DEPTH, n_col),
        in_specs=[
            pl.BlockSpec((bp, D_MODEL), lambda l, j: (0, 0)),
            pl.BlockSpec((1, D_MODEL, D_MODEL), lambda l, j: (l, 0, j)),
            pl.BlockSpec((1, 1, D_MODEL), lambda l, j: (l, 0, j)),
        ],
        out_specs=pl.BlockSpec((1, bp, D_MODEL), lambda l, j: (l, 0, j)),
        out_shape=jax.ShapeDtypeStruct((DEPTH, bp, 6 * D_MODEL), F32),
        compiler_params=_params(("arbitrary", "arbitrary")),
        name="ada_mod",
    )(c, w_ada, b_ada.reshape(DEPTH, 1, 6 * D_MODEL))


def _inproj_kernel(x_ref, g_ref, sc_ref, sh_ref, w_ref, o_ref, h_scr):
    @pl.when(pl.program_id(1) == 0)
    def _():
        y = _rms(x_ref[...], g_ref[...])
        h_scr[...] = (y * (1.0 + sc_ref[0]) + sh_ref[0]).astype(BF16)

    o_ref[...] = jnp.dot(h_scr[...], w_ref[0], preferred_element_type=F32).astype(BF16)


def _inproj(x, g, sc, sh, w, seq, tm):
    n = x.shape[0]
    per_b = seq // tm
    return pl.pallas_call(
        _inproj_kernel,
        grid=(n // tm, IN_COLS // IN_TILE_N),
        in_specs=[
            pl.BlockSpec((tm, D_MODEL), lambda i, j: (i, 0)),
            pl.BlockSpec((1, D_MODEL), lambda i, j: (0, 0)),
            pl.BlockSpec((1, 1, D_MODEL), lambda i, j: (i // per_b, 0, 0)),
            pl.BlockSpec((1, 1, D_MODEL), lambda i, j: (i // per_b, 0, 0)),
            pl.BlockSpec((1, D_MODEL, IN_TILE_N), lambda i, j: (j, 0, 0)),
        ],
        out_specs=pl.BlockSpec((tm, IN_TILE_N), lambda i, j: (i, j)),
        out_shape=jax.ShapeDtypeStruct((n, IN_COLS), BF16),
        scratch_shapes=[pltpu.VMEM((tm, D_MODEL), BF16)],
        compiler_params=_params(("parallel", "arbitrary")),
        name="in_proj",
    )(x, g, sc, sh, w)


def _rope_kernel(aq_ref, ak_ref, av_ref, cos_ref, sin_ref, gq_ref, gk_ref, bd_ref,
                 qh_ref, ka_ref, kb_ref, vt_ref, nrm_ref):
    tm = aq_ref.shape[0]
    lane = lax.broadcasted_iota(jnp.int32, (tm, LANES), 1)
    first_half = (lane % (2 * ROPE_F)) < ROPE_F
    low = lane < HEAD_DIM
    cos = cos_ref[...]
    sin = sin_ref[...]
    bd = bd_ref[...]

    def norm_rope(x, g):
        xx = x * x
        hi = xx.astype(BF16)
        lo = (xx - hi.astype(F32)).astype(BF16)
        ms = (jnp.dot(hi, bd, preferred_element_type=F32)
              + jnp.dot(lo, bd, preferred_element_type=F32))
        y = x * lax.rsqrt(ms + EPS) * g
        nxt = pltpu.roll(y, LANES - ROPE_F, 1)
        prv = pltpu.roll(y, ROPE_F, 1)
        return y * cos + jnp.where(first_half, nxt, prv) * sin

    def max_sq_norm(y):
        ss = jnp.dot((y * y).astype(BF16), bd, preferred_element_type=F32) * HEAD_DIM
        return jnp.max(ss, axis=0, keepdims=True)

    gq = gq_ref[...]
    scale = HEAD_DIM ** -0.5 * float(np.log2(np.e))
    nrm_ref[...] = jnp.zeros(nrm_ref.shape, F32)
    for j in range(ATT_Q // LANES):
        r = norm_rope(aq_ref[:, j * LANES:(j + 1) * LANES].astype(F32), gq) * scale
        qh_ref[:, (2 * j) * LANES:(2 * j + 1) * LANES] = jnp.where(low, r, 0.0).astype(BF16)
        qh_ref[:, (2 * j + 1) * LANES:(2 * j + 2) * LANES] = jnp.where(low, 0.0, r).astype(BF16)
        nrm_ref[0, j:j + 1, :] = max_sq_norm(r)

    k = norm_rope(ak_ref[...].astype(F32), gk_ref[...])
    nrm_ref[0, NRM_K_ROW:NRM_K_ROW + 1, :] = max_sq_norm(k)
    ka_ref[...] = k.astype(BF16)
    kb_ref[...] = pltpu.roll(k, HEAD_DIM, 1).astype(BF16)
    vt = av_ref[...].astype(F32).T.astype(BF16)
    ones = jnp.ones((VT_ROWS - HEAD_DIM, tm), BF16)
    for g in range(N_KV_HEADS):
        vt_ref[g, 0:HEAD_DIM, :] = vt[g * HEAD_DIM:(g + 1) * HEAD_DIM, :]
        vt_ref[g, HEAD_DIM:VT_ROWS, :] = ones


def _rope(proj, cos_t, sin_t, gq, gk, bd, seq, tm):
    n = proj.shape[0]
    per_b = seq // tm
    kv_spec = pl.BlockSpec((tm, LANES), lambda i: (i, 0))
    vt_spec = pl.BlockSpec((N_KV_HEADS, VT_ROWS, tm), lambda i: (0, 0, i))
    return pl.pallas_call(
        _rope_kernel,
        grid=(n // tm,),
        in_specs=[
            pl.BlockSpec((tm, ATT_Q), lambda i: (i, COL_AQ // ATT_Q)),
            pl.BlockSpec((tm, LANES), lambda i: (i, COL_AK // LANES)),
            pl.BlockSpec((tm, LANES), lambda i: (i, COL_AV // LANES)),
            pl.BlockSpec((tm, LANES), lambda i: (i % per_b, 0)),
            pl.BlockSpec((tm, LANES), lambda i: (i % per_b, 0)),
            pl.BlockSpec((1, LANES), lambda i: (0, 0)),
            pl.BlockSpec((1, LANES), lambda i: (0, 0)),
            pl.BlockSpec((LANES, LANES), lambda i: (0, 0)),
        ],
        out_specs=[pl.BlockSpec((tm, N_HEADS * LANES), lambda i: (i, 0)),
                   kv_spec, kv_spec, vt_spec,
                   pl.BlockSpec((1, SUBLANES, LANES), lambda i: (i, 0, 0))],
        out_shape=[jax.ShapeDtypeStruct((n, N_HEADS * LANES), BF16)]
        + [jax.ShapeDtypeStruct((n, LANES), BF16)] * 2
        + [jax.ShapeDtypeStruct((N_KV_HEADS, VT_ROWS, n), BF16),
           jax.ShapeDtypeStruct((n // tm, SUBLANES, LANES), F32)],
        compiler_params=_params(("parallel",)),
        name="qk_norm_rope",
    )(proj, proj, proj, cos_t, sin_t, gq, gk, bd)


def _attn_kernel(bounded_ref, q_ref, ka_ref, kb_ref, vt_ref, o_ref, m_scr, acc_scr):
    j = pl.program_id(2)
    step = (pl.program_id(0) * pl.num_programs(1) + pl.program_id(1)) * pl.num_programs(2) + j

    @pl.when(j == 0)
    def _():
        m_scr[...] = jnp.full(m_scr.shape, -jnp.inf, F32)
        acc_scr[...] = jnp.zeros(acc_scr.shape, F32)

    def scores(h):
        g, half = h // KV_GROUP, h % 2
        k = (ka_ref if g == half else kb_ref)[...]
        q = q_ref[:, h * LANES:(h + 1) * LANES]
        return lax.dot_general(k, q, (((1,), (1,)), ((), ())), preferred_element_type=F32)

    def all_heads(tile_max):
        st_next = scores(0)
        for h in range(N_HEADS):
            vt = vt_ref[h // KV_GROUP]
            st = st_next
            if h + 1 < N_HEADS:
                st_next = scores(h + 1)
            m_prev = m_scr[h]
            m_new = jnp.maximum(m_prev, tile_max(st))
            alpha = jnp.exp2(m_prev - m_new)
            pt = jnp.exp2((st - m_new).astype(BF16))
            acc_scr[h] = alpha * acc_scr[h] + jnp.dot(vt, pt, preferred_element_type=F32)
            m_scr[h] = m_new

    @pl.when(bounded_ref[step] != 0)
    def _():
        all_heads(lambda st: 0.0)

    @pl.when(bounded_ref[step] == 0)
    def _():
        all_heads(lambda st: jnp.max(st, axis=0, keepdims=True))

    @pl.when(j == pl.num_programs(2) - 1)
    def _():
        for jj in range(N_HEADS // 2):
            pair = []
            for h in (2 * jj, 2 * jj + 1):
                acc = acc_scr[h]
                pair.append(acc[0:HEAD_DIM, :] / acc[HEAD_DIM:HEAD_DIM + 1, :])
            o_ref[:, jj * LANES:(jj + 1) * LANES] = jnp.concatenate(pair, axis=0).T.astype(BF16)


def _attention(qh, ka, kb, vt, nrm, batch, seq, tq, tk):
    n = qh.shape[0]
    nq, nk = seq // tq, seq // tk
    halves = jnp.array([0, HEAD_DIM])
    qn2 = nrm[:, :NRM_K_ROW, :][:, :, halves].reshape(batch, nq, -1, N_HEADS).max(axis=2)
    qn2 = qn2.reshape(batch, nq, 1, N_HEADS)
    kn2 = jnp.repeat(nrm[:, NRM_K_ROW, :][:, halves], KV_GROUP, axis=1)
    kn2 = kn2.reshape(batch, nk, -1, N_HEADS).max(axis=2).reshape(batch, 1, nk, N_HEADS)
    bounded = jnp.all(qn2 * kn2 <= SCORE_BOUND ** 2, axis=-1).astype(jnp.int32).reshape(-1)
    k_spec = pl.BlockSpec((tk, LANES), lambda b, i, j, fl: (b * nk + j, 0))
    vt_spec = pl.BlockSpec((N_KV_HEADS, VT_ROWS, tk), lambda b, i, j, fl: (0, 0, b * nk + j))
    grid_spec = pltpu.PrefetchScalarGridSpec(
        num_scalar_prefetch=1,
        grid=(batch, nq, nk),
        in_specs=[pl.BlockSpec((tq, N_HEADS * LANES), lambda b, i, j, fl: (b * nq + i, 0)),
                  k_spec, k_spec, vt_spec],
        out_specs=pl.BlockSpec((tq, ATT_Q), lambda b, i, j, fl: (b * nq + i, 0)),
        scratch_shapes=[pltpu.VMEM((N_HEADS, 1, tq), F32),
                        pltpu.VMEM((N_HEADS, VT_ROWS, tq), F32)],
    )
    return pl.pallas_call(
        _attn_kernel,
        grid_spec=grid_spec,
        out_shape=jax.ShapeDtypeStruct((n, ATT_Q), BF16),
        compiler_params=_params(("parallel", "parallel", "arbitrary")),
        name="gqa_attention",
    )(bounded, qh, ka, kb, vt)


def _level_ref(b, m, rev):
    c = b.shape[0]
    g = 2 * m
    if g >= SUBLANES:
        b3 = b.reshape(c // g, g, LANES)
        r = m if rev else m - 1
        return jnp.broadcast_to(b3[:, r:r + 1, :], b3.shape).reshape(c, LANES)
    b3 = b.reshape(c // SUBLANES, SUBLANES, LANES)
    sub = lax.broadcasted_iota(jnp.int32, b3.shape, 1)
    out = None
    for p0 in range(0, SUBLANES, g):
        r = p0 + (m if rev else m - 1)
        cand = jnp.broadcast_to(b3[:, r:r + 1, :], b3.shape)
        out = cand if out is None else jnp.where(sub >= p0, cand, out)
    return out.reshape(c, LANES)


def _chunk_cumsum(x, tri):
    hi = x.astype(BF16)
    r1 = x - hi.astype(F32)
    mid = r1.astype(BF16)
    lo = (r1 - mid.astype(F32)).astype(BF16)
    return (jnp.dot(tri, hi, preferred_element_type=F32)
            + jnp.dot(tri, mid, preferred_element_type=F32)
            + jnp.dot(tri, lo, preferred_element_type=F32))


def _neg_abs(z):
    bits = lax.bitcast_convert_type(z, jnp.int32) | jnp.int32(-2 ** 31)
    return lax.bitcast_convert_type(bits, F32)


def _hgrn_kernel(hq_ref, hf_ref, hi_ref, lbl_ref, o_ref, st_scr, *, rev, layer, chunk):
    c = chunk
    n_chunks = hq_ref.shape[0] // c

    @pl.when(pl.program_id(1) == 0)
    def _():
        st_scr[...] = jnp.zeros(st_scr.shape, F32)

    lbl = lbl_ref[...]
    lbe = jnp.exp(lbl - jnp.max(lbl, axis=0, keepdims=True))
    lbp = lbe / jnp.sum(lbe, axis=0, keepdims=True)
    lb_all = jnp.zeros((1, HG_F), F32)
    for l in range(1, layer + 1):
        lb_all = lb_all + lbp[l:l + 1, :]

    row = lax.broadcasted_iota(jnp.int32, (c, LANES), 0)
    row_a = lax.broadcasted_iota(jnp.int32, (c, c), 0)
    col_a = lax.broadcasted_iota(jnp.int32, (c, c), 1)
    tri = jnp.where((row_a <= col_a) if rev else (row_a >= col_a), 1.0, 0.0).astype(BF16)
    lev = 31 - lax.clz(row_a ^ col_a)
    lev = jnp.where((row_a < col_a) if rev else (row_a > col_a), lev, -1)
    q_scale = HG_DK ** -0.5
    log2e = float(np.log2(np.e))
    nt_dims = (((1,), (1,)), ((), ()))

    def body(ci, carry):
        cc = (n_chunks - 1 - ci) if rev else ci
        r0 = pl.multiple_of(cc * c, c)
        heads = range(HG_HEADS)
        cols = [slice(h * LANES, (h + 1) * LANES) for h in heads]
        v = [hi_ref[pl.ds(r0, c), cols[h]] for h in heads]
        q = [_silu(hq_ref[pl.ds(r0, c), cols[h]].astype(F32)) * q_scale for h in heads]
        f = [lb_all[:, cols[h]] + (1.0 - lb_all[:, cols[h]])
             * _sigmoid(hf_ref[pl.ds(r0, c), cols[h]].astype(F32)) for h in heads]
        k = [1.0 - f[h] for h in heads]
        b = [_chunk_cumsum(jnp.log(f[h]) * log2e, tri) for h in heads]
        a = [jnp.zeros((c, c), F32) for h in heads]
        m = c // 2
        while m >= 1:
            upper = (row & m) != 0
            for h in heads:
                e = jnp.exp2(_neg_abs(b[h] - _level_ref(b[h], m, rev)).astype(BF16))
                w = jnp.where(upper, k[h], q[h]) if rev else jnp.where(upper, q[h], k[h])
                w = w.astype(BF16) * e
                al = lax.dot_general(w, w, nt_dims, preferred_element_type=F32)
                a[h] = jnp.where(lev == (m.bit_length() - 1), al, a[h])
            m //= 2
        for h in heads:
            st = st_scr[h]
            o = lax.dot_general((q[h] * jnp.exp2(b[h])).astype(BF16), st.astype(BF16), nt_dims,
                                preferred_element_type=F32)
            diag = jnp.sum(q[h] * k[h], axis=-1, keepdims=True)
            o = (o + jnp.dot(a[h].astype(BF16), v[h], preferred_element_type=F32)
                 + diag * v[h].astype(F32))
            o_ref[pl.ds(r0, c), cols[h]] = o.astype(o_ref.dtype)
            b_end = b[h][0:1, :] if rev else b[h][c - 1:c, :]
            k_end = (k[h] * jnp.exp2(b_end - b[h])).astype(BF16)
            upd = lax.dot_general(v[h], k_end, (((0,), (0,)), ((), ())), preferred_element_type=F32)
            st_scr[h] = st * jnp.exp2(b_end) + upd
        return carry

    lax.fori_loop(0, n_chunks, body, 0, unroll=True)


def _hgrn(proj, lb_logits_dir, batch, seq, tt, rev, layer):
    n = proj.shape[0]
    nt = seq // tt
    col_f = (COL_HFB if rev else COL_HFF) // HG_F

    def rows(b, i):
        return b * nt + ((nt - 1 - i) if rev else i)

    return pl.pallas_call(
        functools.partial(_hgrn_kernel, rev=rev, layer=layer, chunk=min(HG_CHUNK, tt)),
        grid=(batch, nt),
        in_specs=[
            pl.BlockSpec((tt, HG_F), lambda b, i: (rows(b, i), COL_HQ // HG_F)),
            pl.BlockSpec((tt, HG_F), lambda b, i: (rows(b, i), col_f)),
            pl.BlockSpec((tt, HG_F), lambda b, i: (rows(b, i), COL_HI // HG_F)),
            pl.BlockSpec((DEPTH, HG_F), lambda b, i: (0, 0)),
        ],
        out_specs=pl.BlockSpec((tt, HG_F), lambda b, i: (rows(b, i), 0)),
        out_shape=jax.ShapeDtypeStruct((n, HG_F), BF16),
        scratch_shapes=[pltpu.VMEM((HG_HEADS, HG_DV, HG_DK), F32)],
        compiler_params=_params(("parallel", "arbitrary")),
        name="hgrn2_bwd" if rev else "hgrn2_fwd",
    )(proj, proj, proj, lb_logits_dir)


def _mixout_kernel(att_ref, of_ref, ob_ref, hg_ref, ga_ref, gb_ref, x_ref, gt_ref, ghg_ref,
                   gn_ref, wa_ref, wb_ref, wo_ref, o_ref):
    ghg = ghg_ref[...]
    recs = []
    for h in range(HG_HEADS):
        cols = slice(h * LANES, (h + 1) * LANES)
        o = of_ref[:, cols].astype(F32) + ob_ref[:, cols].astype(F32)
        recs.append((_rms(o, ghg) * _silu(hg_ref[:, cols].astype(F32))).astype(BF16))
    rec = jnp.concatenate(recs, axis=1)
    pa = jnp.dot(att_ref[...], wa_ref[...], preferred_element_type=F32)
    pb = jnp.dot(rec, wb_ref[...], preferred_element_type=F32)
    merged = (_sigmoid(ga_ref[...].astype(F32)) * pa + _sigmoid(gb_ref[...].astype(F32)) * pb)
    y = jnp.dot(merged.astype(BF16), wo_ref[...], preferred_element_type=F32)
    o_ref[...] = x_ref[...] + gt_ref[0] * _rms(y, gn_ref[...])


def _mixout(att, o_f, o_b, proj, x, gt, g_hgrn, g_norm, w_a, w_b, w_o, seq, tm):
    n = x.shape[0]
    per_b = seq // tm
    half = pl.BlockSpec((tm, ATT_Q), lambda i: (i, 0))
    const = lambda shape: pl.BlockSpec(shape, lambda i: (0,) * len(shape))
    return pl.pallas_call(
        _mixout_kernel,
        grid=(n // tm,),
        in_specs=[
            half, half, half,
            pl.BlockSpec((tm, HG_F), lambda i: (i, COL_HG // HG_F)),
            pl.BlockSpec((tm, D_MODEL), lambda i: (i, COL_GA // D_MODEL)),
            pl.BlockSpec((tm, D_MODEL), lambda i: (i, COL_GB // D_MODEL)),
            pl.BlockSpec((tm, D_MODEL), lambda i: (i, 0)),
            pl.BlockSpec((1, 1, D_MODEL), lambda i: (i // per_b, 0, 0)),
            const((1, LANES)), const((1, D_MODEL)),
            const((ATT_Q, D_MODEL)), const((HG_F, D_MODEL)), const((D_MODEL, D_MODEL)),
        ],
        out_specs=pl.BlockSpec((tm, D_MODEL), lambda i: (i, 0)),
        out_shape=jax.ShapeDtypeStruct((n, D_MODEL), F32),
        compiler_params=_params(("parallel",)),
        name="mixer_out",
    )(att, o_f, o_b, proj, proj, proj, x, gt, g_hgrn, g_norm, w_a, w_b, w_o)


def _router_kernel(x_ref, g_ref, sc_ref, sh_ref, wr_ref, br_ref, h_ref, sel_ref):
    h = _rms(x_ref[...], g_ref[...]) * (1.0 + sc_ref[0]) + sh_ref[0]
    tm = h.shape[0]
    for j in range(D_MODEL // LANES):
        h_ref[pl.ds(j, tm, stride=SUBLANES), :] = h[:, j * LANES:(j + 1) * LANES]
    logits = jnp.dot(h, wr_ref[...], preferred_element_type=F32,
                     precision=lax.Precision.HIGHEST) + br_ref[...]
    lane = lax.broadcasted_iota(jnp.int32, logits.shape, 1).astype(F32)
    logits = jnp.where(lane < N_EXPERTS, logits, -jnp.inf)
    m1 = jnp.max(logits, axis=-1, keepdims=True)
    i1 = jnp.min(jnp.where(logits == m1, lane, float(LANES)), axis=-1, keepdims=True)
    rest = jnp.where(lane == i1, -jnp.inf, logits)
    m2 = jnp.max(rest, axis=-1, keepdims=True)
    i2 = jnp.min(jnp.where(rest == m2, lane, float(LANES)), axis=-1, keepdims=True)
    e2 = jnp.exp(m2 - m1)
    w1 = 1.0 / (1.0 + e2)
    w2 = e2 / (1.0 + e2)
    sel_ref[...] = (jnp.where(lane == SEL_E1, i1, 0.0) + jnp.where(lane == SEL_E2, i2, 0.0)
                    + jnp.where(lane == SEL_W1, w1, 0.0) + jnp.where(lane == SEL_W2, w2, 0.0))


def _router(x, g, sc, sh, w_r, b_r, seq, tm):
    n = x.shape[0]
    per_b = seq // tm
    return pl.pallas_call(
        _router_kernel,
        grid=(n // tm,),
        in_specs=[
            pl.BlockSpec((tm, D_MODEL), lambda i: (i, 0)),
            pl.BlockSpec((1, D_MODEL), lambda i: (0, 0)),
            pl.BlockSpec((1, 1, D_MODEL), lambda i: (i // per_b, 0, 0)),
            pl.BlockSpec((1, 1, D_MODEL), lambda i: (i // per_b, 0, 0)),
            pl.BlockSpec((D_MODEL, LANES), lambda i: (0, 0)),
            pl.BlockSpec((1, LANES), lambda i: (0, 0)),
        ],
        out_specs=[pl.BlockSpec((tm * SUBLANES, LANES), lambda i: (i, 0)),
                   pl.BlockSpec((tm, LANES), lambda i: (i, 0))],
        out_shape=[jax.ShapeDtypeStruct((n * SUBLANES, LANES), F32),
                   jax.ShapeDtypeStruct((n, LANES), F32)],
        compiler_params=_params(("parallel",)),
        name="router_top2",
    )(x, g, sc, sh, w_r, b_r)


def _route_plan(sel, tm, tmc):
    n = sel.shape[0]
    r = 2 * n
    flat_e = sel[:, :2].astype(jnp.int32).reshape(-1)
    onehot = (flat_e[:, None] == jnp.arange(N_EXPERTS, dtype=jnp.int32)[None, :]).astype(jnp.int32)
    csum = jnp.cumsum(onehot, axis=0)
    rank = jnp.sum(csum * onehot, axis=1) - 1
    counts = csum[-1]
    padded = (counts + tm - 1) // tm * tm
    ends = jnp.cumsum(padded)
    pos = (ends - padded)[flat_e] + rank
    n_tiles = r // tm + N_EXPERTS
    n_valid = (ends[-1] // tm).astype(jnp.int32)
    tile_start = jnp.arange(n_tiles, dtype=jnp.int32) * tm
    texp = jnp.minimum(jnp.searchsorted(ends, tile_start, side="right"), N_EXPERTS - 1)
    texp = jnp.where(tile_start < ends[-1], texp, texp[jnp.maximum(n_valid - 1, 0)])
    order = jnp.argsort(flat_e, stable=True).astype(jnp.int32)
    row_e = jnp.repeat(texp, tm)
    local = jnp.arange(n_tiles * tm, dtype=jnp.int32) - (ends - padded)[row_e]
    sorted_idx = jnp.clip((jnp.cumsum(counts) - counts)[row_e] + local, 0, r - 1)
    src = jnp.where(local < counts[row_e], order[sorted_idx] // 2, 0)
    src = src * SUBLANES
    pos = pos * SUBLANES
    pos_tiles = pos.reshape(n // tmc, tmc, 2).transpose(0, 2, 1).reshape(n // tmc, 2 * tmc)
    return src.reshape(n_tiles, tm), texp.astype(jnp.int32), n_valid.reshape(1), pos_tiles


def _moe_kernel(texp_ref, nv_ref, src_hbm, h_hbm, wa_ref, wb_ref, wo_ref, y_ref,
                hbuf, hb16, acc, idx_smem, gsem, isem, *, tm):
    t = pl.program_id(0)
    f = pl.program_id(1)
    nv = nv_ref[0]
    slot = t % 2
    other = 1 - slot
    n_lane_blocks = D_MODEL // LANES

    def idx_copy(tile, s):
        return pltpu.make_async_copy(src_hbm.at[pl.ds(tile, 1)], idx_smem.at[pl.ds(s, 1)], isem.at[s])

    def issue_row(s, r):
        src_row = pl.multiple_of(idx_smem[s, r], SUBLANES)
        dst_row = pl.multiple_of(r * SUBLANES, SUBLANES)
        pltpu.make_async_copy(h_hbm.at[pl.ds(src_row, SUBLANES)],
                              hbuf.at[s, pl.ds(dst_row, SUBLANES)], gsem.at[s]).start()

    def wait_rows(s):
        pltpu.make_async_copy(h_hbm.at[pl.ds(0, tm * SUBLANES)], hbuf.at[s], gsem.at[s]).wait()

    def swiglu_chunk():
        h = hb16[...]
        a = jnp.dot(h, wa_ref[0, 0], preferred_element_type=F32)
        b = jnp.dot(h, wb_ref[0, 0], preferred_element_type=F32)
        return jnp.dot((_silu(a) * b).astype(BF16), wo_ref[0], preferred_element_type=F32)

    @pl.when((t == 0) & (f == 0))
    def _():
        first = idx_copy(0, 0)
        first.start()
        first.wait()
        lax.fori_loop(0, tm, lambda r, c: (issue_row(0, r), c)[1], 0, unroll=8)
        idx_copy(1, 1).start()

    @pl.when((f == 0) & (t <= nv))
    def _():
        wait_rows(slot)
        for j in range(n_lane_blocks):
            hb16[:, j * LANES:(j + 1) * LANES] = (
                hbuf[slot, pl.ds(j, tm, stride=SUBLANES), :].astype(BF16))

    @pl.when((f == 0) & (t < nv))
    def _():
        idx_copy(t + 1, other).wait()
        for r in range(tm):
            issue_row(other, r)
        acc[...] = swiglu_chunk()

    @pl.when((f == 1) & (t < nv))
    def _():
        @pl.when(t + 1 < nv)
        def _():
            idx_copy(t + 2, slot).start()

        y = swiglu_chunk()
        for j in range(n_lane_blocks):
            cols = slice(j * LANES, (j + 1) * LANES)
            y_ref[pl.ds(j, tm, stride=SUBLANES), :] = acc[:, cols] + y[:, cols]

    @pl.when((t >= nv) & (f == 0))
    def _():
        y_ref[...] = jnp.zeros(y_ref.shape, F32)


def _moe_experts(h, src, texp, n_valid, w_in, w_out, tm, chunk):
    n_tiles = src.shape[0]
    n_fc = w_in.shape[1] // 2
    assert n_fc == 2, "the kernel's gather schedule and accumulation are written for two chunks"

    def fidx(t, f, nv):
        return jnp.where(t < nv[0], f, n_fc - 1)

    grid_spec = pltpu.PrefetchScalarGridSpec(
        num_scalar_prefetch=2,
        grid=(n_tiles, n_fc),
        in_specs=[
            pl.BlockSpec(memory_space=pl.ANY),
            pl.BlockSpec(memory_space=pl.ANY),
            pl.BlockSpec((1, 1, D_MODEL, chunk), lambda t, f, te, nv: (te[t], fidx(t, f, nv), 0, 0)),
            pl.BlockSpec((1, 1, D_MODEL, chunk),
                         lambda t, f, te, nv: (te[t], n_fc + fidx(t, f, nv), 0, 0)),
            pl.BlockSpec((1, chunk, D_MODEL), lambda t, f, te, nv: (te[t], fidx(t, f, nv), 0)),
        ],
        out_specs=pl.BlockSpec((tm * SUBLANES, LANES), lambda t, f, te, nv: (t, 0)),
        scratch_shapes=[
            pltpu.VMEM((2, tm * SUBLANES, LANES), F32),
            pltpu.VMEM((tm, D_MODEL), BF16),
            pltpu.VMEM((tm, D_MODEL), F32),
            pltpu.SMEM((2, tm), jnp.int32),
            pltpu.SemaphoreType.DMA((2,)),
            pltpu.SemaphoreType.DMA((2,)),
        ],
    )
    return pl.pallas_call(
        functools.partial(_moe_kernel, tm=tm),
        grid_spec=grid_spec,
        out_shape=jax.ShapeDtypeStruct((n_tiles * tm * SUBLANES, LANES), F32),
        compiler_params=_params(("arbitrary", "arbitrary")),
        name="moe_experts",
    )(texp, n_valid, src, h, w_in, w_in, w_out)


def _combine_kernel(pos_hbm, y_hbm, x_ref, sel_ref, gt_ref, gn_ref, o_ref, ybuf, idx_smem, gsem, isem,
                    *, tmc, n_steps):
    i = pl.program_id(0)
    slot = i % 2
    other = 1 - slot
    rows = 2 * tmc

    def idx_copy(tile, s):
        return pltpu.make_async_copy(pos_hbm.at[pl.ds(tile, 1)], idx_smem.at[pl.ds(s, 1)], isem.at[s])

    def issue_rows(s):
        def body(r2, carry):
            for p in range(2):
                r = 2 * r2 + p
                src_row = pl.multiple_of(idx_smem[s, r], SUBLANES)
                dst_row = pl.multiple_of(r * SUBLANES, SUBLANES)
                pltpu.make_async_copy(y_hbm.at[pl.ds(src_row, SUBLANES)],
                                      ybuf.at[s, pl.ds(dst_row, SUBLANES)],
                                      gsem.at[s]).start(priority=p)
            return carry
        lax.fori_loop(0, rows // 2, body, 0, unroll=4)

    @pl.when(i == 0)
    def _():
        first = idx_copy(0, 0)
        first.start()
        first.wait()
        issue_rows(0)
        if n_steps > 1:
            idx_copy(1, 1).start()

    @pl.when(i + 1 < n_steps)
    def _():
        idx_copy(i + 1, other).wait()
        issue_rows(other)

        @pl.when(i + 2 < n_steps)
        def _():
            idx_copy(i + 2, slot).start()

    pltpu.make_async_copy(y_hbm.at[pl.ds(0, rows * SUBLANES)], ybuf.at[slot], gsem.at[slot]).wait()

    sel = sel_ref[...]
    w1 = sel[:, SEL_W1:SEL_W1 + 1]
    w2 = sel[:, SEL_W2:SEL_W2 + 1]
    second = tmc * SUBLANES
    y = jnp.concatenate(
        [w1 * ybuf[slot, pl.ds(j, tmc, stride=SUBLANES), :]
         + w2 * ybuf[slot, pl.ds(second + j, tmc, stride=SUBLANES), :]
         for j in range(D_MODEL // LANES)], axis=1)
    o_ref[...] = x_ref[...] + gt_ref[0] * _rms(y, gn_ref[...])


def _combine(pos_tiles, y_sorted, x, sel, gt, gn, seq, tmc):
    n = x.shape[0]
    per_b = seq // tmc
    return pl.pallas_call(
        functools.partial(_combine_kernel, tmc=tmc, n_steps=n // tmc),
        grid=(n // tmc,),
        in_specs=[
            pl.BlockSpec(memory_space=pl.ANY),
            pl.BlockSpec(memory_space=pl.ANY),
            pl.BlockSpec((tmc, D_MODEL), lambda i: (i, 0)),
            pl.BlockSpec((tmc, LANES), lambda i: (i, 0)),
            pl.BlockSpec((1, 1, D_MODEL), lambda i: (i // per_b, 0, 0)),
            pl.BlockSpec((1, D_MODEL), lambda i: (0, 0)),
        ],
        out_specs=pl.BlockSpec((tmc, D_MODEL), lambda i: (i, 0)),
        out_shape=jax.ShapeDtypeStruct((n, D_MODEL), F32),
        scratch_shapes=[
            pltpu.VMEM((2, 2 * tmc * SUBLANES, LANES), F32),
            pltpu.SMEM((2, 2 * tmc), jnp.int32),
            pltpu.SemaphoreType.DMA((2,)),
            pltpu.SemaphoreType.DMA((2,)),
        ],
        compiler_params=_params(("arbitrary",)),
        name="moe_combine",
    )(pos_tiles, y_sorted, x, sel, gt, gn)


def _ffn_kernel(x_ref, g_ref, sc_ref, sh_ref, gt_ref, gn_ref, wa_ref, wb_ref, wo_ref,
                o_ref, h_scr, acc_scr):
    f = pl.program_id(1)

    @pl.when(f == 0)
    def _():
        y = _rms(x_ref[...], g_ref[...])
        h_scr[...] = (y * (1.0 + sc_ref[0]) + sh_ref[0]).astype(BF16)
        acc_scr[...] = jnp.zeros(acc_scr.shape, F32)

    h = h_scr[...]
    a = jnp.dot(h, wa_ref[0], preferred_element_type=F32)
    b = jnp.dot(h, wb_ref[0], preferred_element_type=F32)
    acc_scr[...] += jnp.dot((_silu(a) * b).astype(BF16), wo_ref[...], preferred_element_type=F32)

    @pl.when(f == pl.num_programs(1) - 1)
    def _():
        o_ref[...] = x_ref[...] + gt_ref[0] * _rms(acc_scr[...], gn_ref[...])


def _ffn(x, g, sc, sh, gt, gn, w_in, w_out, seq, tm, chunk):
    n = x.shape[0]
    per_b = seq // tm
    n_fc = w_in.shape[0] // 2
    vec = pl.BlockSpec((1, D_MODEL), lambda i, f: (0, 0))
    mod = pl.BlockSpec((1, 1, D_MODEL), lambda i, f: (i // per_b, 0, 0))
    return pl.pallas_call(
        _ffn_kernel,
        grid=(n // tm, n_fc),
        in_specs=[
            pl.BlockSpec((tm, D_MODEL), lambda i, f: (i, 0)), vec, mod, mod, mod, vec,
            pl.BlockSpec((1, D_MODEL, chunk), lambda i, f: (f, 0, 0)),
            pl.BlockSpec((1, D_MODEL, chunk), lambda i, f: (n_fc + f, 0, 0)),
            pl.BlockSpec((chunk, D_MODEL), lambda i, f: (f, 0)),
        ],
        out_specs=pl.BlockSpec((tm, D_MODEL), lambda i, f: (i, 0)),
        out_shape=jax.ShapeDtypeStruct((n, D_MODEL), F32),
        scratch_shapes=[pltpu.VMEM((tm, D_MODEL), BF16), pltpu.VMEM((tm, D_MODEL), F32)],
        compiler_params=_params(("parallel", "arbitrary")),
        name="dense_ffn",
    )(x, g, sc, sh, gt, gn, w_in, w_in, w_out)


def _rope_tables(seq):
    t = jnp.arange(seq, dtype=jnp.int32)
    row = (t // GRID_W).astype(F32)
    col = (t % GRID_W).astype(F32)
    inv = ROPE_THETA ** (-jnp.arange(ROPE_F, dtype=F32) / ROPE_F)
    ang_r = row[:, None] * inv
    ang_c = col[:, None] * inv
    cos = jnp.concatenate([jnp.cos(ang_r)] * 2 + [jnp.cos(ang_c)] * 2, axis=1)
    sin = jnp.concatenate([-jnp.sin(ang_r), jnp.sin(ang_r), -jnp.sin(ang_c), jnp.sin(ang_c)], axis=1)
    return jnp.tile(cos, (1, 2)), jnp.tile(sin, (1, 2))


def _prep_weights(w_in, g_qk, w_branch, w_o, w_ffn_in, w_ffn_out, w_router, b_router,
                  w_moe_in, w_moe_out):
    o_aq, o_ak, o_av, o_hq = 0, ATT_Q, ATT_Q + ATT_KV, ATT_Q + 2 * ATT_KV
    o_ga = o_hq + 5 * HG_F
    w_in_p = jnp.concatenate(
        [w_in[:, :, o_ga:], w_in[:, :, o_hq:o_ga], w_in[:, :, o_aq:o_hq]], axis=2).astype(BF16)
    pad_c = D_FF_PAD - D_FF
    w_fa = jnp.pad(w_ffn_in[:, :, :D_FF], ((0, 0), (0, 0), (0, pad_c)))
    w_fb = jnp.pad(w_ffn_in[:, :, D_FF:], ((0, 0), (0, 0), (0, pad_c)))
    w_ffn_in_p = jnp.concatenate([w_fa, w_fb], axis=2).astype(BF16)
    w_ffn_out_p = jnp.pad(w_ffn_out, ((0, 0), (0, pad_c), (0, 0))).astype(BF16)
    w_r_p = jnp.pad(w_router, ((0, 0), (0, 0), (0, LANES - N_EXPERTS)))
    b_r_p = jnp.pad(b_router, ((0, 0), (0, LANES - N_EXPERTS)))[:, None, :]
    g_q = jnp.tile(g_qk[:, 0, :], (1, LANES // HEAD_DIM))[:, None, :]
    g_k = jnp.tile(g_qk[:, 1, :], (1, LANES // HEAD_DIM))[:, None, :]

    def col_chunks(w, chunk):
        lead, (rows, cols) = w.shape[:-2], w.shape[-2:]
        return jnp.moveaxis(w.reshape(*lead, rows, cols // chunk, chunk), -2, -3)

    return dict(w_in=col_chunks(w_in_p, IN_TILE_N), w_branch=w_branch.astype(BF16),
                w_o=w_o.astype(BF16), w_ffn_in=col_chunks(w_ffn_in_p, FFN_CHUNK_DENSE),
                w_ffn_out=w_ffn_out_p, w_r=w_r_p, b_r=b_r_p,
                w_moe_in=col_chunks(w_moe_in.astype(BF16), FFN_CHUNK_EXPERT),
                w_moe_out=w_moe_out.astype(BF16), g_q=g_q, g_k=g_k)


def _tile(seq, want):
    return min(seq, want)


def _trunk(x3, mods, g_norm, lb_logits, g_hgrn, wts):
    batch, seq, _ = x3.shape
    x = x3.reshape(batch * seq, D_MODEL)
    cos_t, sin_t = _rope_tables(seq)
    idx = np.arange(LANES)
    bd = jnp.asarray((idx[:, None] // HEAD_DIM == idx[None, :] // HEAD_DIM) / HEAD_DIM, BF16)
    tm_big = _tile(seq, 1024)
    tm = _tile(seq, 512)
    for l in range(DEPTH):
        mod = mods[l]
        sh1, sc1, gt1, sh2, sc2, gt2 = [m[:, None, :] for m in jnp.split(mod, 6, axis=-1)]
        gn = g_norm[l][:, None, :]
        proj = _inproj(x, gn[0], sc1, sh1, wts["w_in"][l], seq, tm_big)
        qh, ka, kb, vt, nrm = _rope(proj, cos_t, sin_t, wts["g_q"][l], wts["g_k"][l], bd, seq, tm)
        att = _attention(qh, ka, kb, vt, nrm, batch, seq, tm, min(seq, ATT_KEY_TILE))
        o_f = _hgrn(proj, lb_logits[0], batch, seq, tm, False, l)
        o_b = _hgrn(proj, lb_logits[1], batch, seq, tm, True, l)
        x = _mixout(att, o_f, o_b, proj, x, gt1, g_hgrn[l][None, :], gn[1],
                    wts["w_branch"][l, 0], wts["w_branch"][l, 1], wts["w_o"][l], seq, tm)
        if l % 2 == 0:
            x = _ffn(x, gn[2], sc2, sh2, gt2, gn[3], wts["w_ffn_in"][l // 2],
                     wts["w_ffn_out"][l // 2], seq, tm, FFN_CHUNK_DENSE)
        else:
            h, sel = _router(x, gn[2], sc2, sh2, wts["w_r"][l // 2], wts["b_r"][l // 2], seq, tm)
            moe_tile = min(MOE_TILE, 2 * batch * seq)
            src, texp, n_valid, pos_tiles = _route_plan(sel, moe_tile, tm)
            y_sorted = _moe_experts(h, src, texp, n_valid, wts["w_moe_in"][l // 2],
                                    wts["w_moe_out"][l // 2], moe_tile, FFN_CHUNK_EXPERT)
            x = _combine(pos_tiles, y_sorted, x, sel, gt2, gn[3], seq, tm)
    return x.reshape(batch, seq, D_MODEL)


def kernel(x_prompt, x_sample, c_prompt, c_sample, w_ada, b_ada, g_norm, w_in, g_qk, lb_logits,
           g_hgrn, w_branch, w_o, w_ffn_in, w_ffn_out, w_router, b_router, w_moe_in, w_moe_out):
    wts = _prep_weights(w_in, g_qk, w_branch, w_o, w_ffn_in, w_ffn_out, w_router, b_router,
                        w_moe_in, w_moe_out)
    n_p, n_s = c_prompt.shape[0], c_sample.shape[0]
    c_all = jnp.concatenate([c_prompt, c_sample], axis=0)
    pad = (-c_all.shape[0]) % SUBLANES
    c_all = jnp.pad(c_all, ((0, pad), (0, 0)))
    mods = _ada(c_all, w_ada, b_ada)
    y_p = _trunk(x_prompt, mods[:, :n_p], g_norm, lb_logits, g_hgrn, wts)
    y_s = _trunk(x_sample, mods[:, n_p:n_p + n_s], g_norm, lb_logits, g_hgrn, wts)
    return (y_p, y_s)
```
